```python
import jax, jax.numpy as jnp
from jax import lax
import numpy as np


D_MODEL = 1024
BATCH = 8
SEQ = 4096
DEPTH = 2

N_HEADS_ATTN = 8
HEAD_DIM = 64
ATTN_WIDTH = N_HEADS_ATTN * HEAD_DIM
ROPE_DIM = HEAD_DIM // 4
ROPE_THETA = 500000.0
IDX_HEADS = 8
IDX_DIM = 64
IDX_ROPE_DIM = IDX_DIM // 4
INDEX_TOPK = 256
Q_BLOCK = 64
LRU_WIDTH = 512
LRU_BLOCKS = 8
LRU_BLOCK_DIM = LRU_WIDTH // LRU_BLOCKS
CONV_WIDTH = 4
LRU_C = 8.0
GMLP_WIDTH = 512
GMLP_GROUPS = 8
GMLP_GROUP_DIM = GMLP_WIDTH // GMLP_GROUPS
CHUNK = 128
N_BRANCH = 3
BRANCH_WIDTH = 512
FFN_DIM = 4 * D_MODEL
PLE_DIM = 256
EPS = 1e-6

IN_SPLITS = (ATTN_WIDTH, ATTN_WIDTH, ATTN_WIDTH, IDX_HEADS * IDX_DIM, IDX_DIM, IDX_HEADS,
             LRU_WIDTH, LRU_WIDTH, GMLP_WIDTH, GMLP_WIDTH, N_BRANCH * D_MODEL)
IN_WIDTH = sum(IN_SPLITS)

kernel_name = 'hybrid_dsa_rglru_gmlp_block'


def rms_norm(x, g):
    xf = x.astype(jnp.float32)
    y = xf * lax.rsqrt(jnp.mean(xf * xf, axis=-1, keepdims=True) + EPS)
    return (y * g.astype(jnp.float32)).astype(x.dtype)


def apply_partial_rope(x, positions, rot_dim):
    half = rot_dim // 2
    inv_freq = ROPE_THETA ** (-jnp.arange(half, dtype=jnp.float32) * 2.0 / rot_dim)
    ang = positions.astype(jnp.float32)[..., None] * inv_freq
    cos = jnp.cos(ang)[:, :, None, :].astype(x.dtype)
    sin = jnp.sin(ang)[:, :, None, :].astype(x.dtype)
    x1 = x[..., :half]
    x2 = x[..., half:rot_dim]
    return jnp.concatenate([x1 * cos - x2 * sin, x2 * cos + x1 * sin, x[..., rot_dim:]], axis=-1)


def dsa_attention(q, k, v, qi, ki, wi):
    B, L = q.shape[0], q.shape[1]
    top_k = min(INDEX_TOPK, L // 4)
    nb = L // Q_BLOCK
    key_pos = jnp.arange(L)
    attn_scale = HEAD_DIM ** -0.5
    idx_scale = (IDX_DIM ** -0.5) * (IDX_HEADS ** -0.5)
    ki32 = ki.astype(jnp.float32)

    def to_blocks(a):
        return jnp.moveaxis(a.reshape((B, nb, Q_BLOCK) + a.shape[2:]), 1, 0)

    def one_block(args):
        qb, qib, wib, qpos = args
        dots = jnp.einsum('bqhd,bsd->bqhs', qib.astype(jnp.float32), ki32)
        score = jnp.einsum('bqh,bqhs->bqs', wib.astype(jnp.float32), jax.nn.relu(dots)) * idx_scale
        causal = key_pos[None, :] <= qpos[:, None]
        score = jnp.where(causal[None], score, -jnp.inf)
        _, sel = lax.top_k(score, top_k)
        valid = sel <= qpos[None, :, None]
        kg = jax.vmap(lambda kb, ib: kb[ib])(k, sel)
        vg = jax.vmap(lambda vb, ib: vb[ib])(v, sel)
        logits = jnp.einsum('bqhd,bqkhd->bqhk', qb, kg).astype(jnp.float32) * attn_scale
        logits = jnp.where(valid[:, :, None, :], logits, -jnp.inf)
        prob = jax.nn.softmax(logits, axis=-1).astype(v.dtype)
        return jnp.einsum('bqhk,bqkhd->bqhd', prob, vg)

    qpos_blocks = jnp.arange(L).reshape(nb, Q_BLOCK)
    out = lax.map(one_block, (to_blocks(q), to_blocks(qi), to_blocks(wi), qpos_blocks))
    return jnp.moveaxis(out, 0, 1).reshape(B, L, ATTN_WIDTH)


def causal_depthwise_conv(x, w, b):
    C = x.shape[-1]
    y = lax.conv_general_dilated(x, w.astype(x.dtype)[:, None, :], window_strides=(1,),
                                 padding=((CONV_WIDTH - 1, 0),),
                                 dimension_numbers=('NWC', 'WIO', 'NWC'),
                                 feature_group_count=C)
    return y + b.astype(x.dtype)


def rg_lru(x, w_a, b_a, w_x, b_x, lam):
    B, L, _ = x.shape
    xb = x.reshape(B, L, LRU_BLOCKS, LRU_BLOCK_DIM)
    r = jax.nn.sigmoid(jnp.einsum('bshi,hij->bshj', xb, w_a).reshape(B, L, LRU_WIDTH) + b_a)
    i = jax.nn.sigmoid(jnp.einsum('bshi,hij->bshj', xb, w_x).reshape(B, L, LRU_WIDTH) + b_x)
    log_a = -LRU_C * r.astype(jnp.float32) * jax.nn.softplus(-lam.astype(jnp.float32))
    a = jnp.exp(log_a)
    inp = jnp.sqrt(-jnp.expm1(2.0 * log_a)) * (i * x).astype(jnp.float32)

    def combine(left, right):
        a1, b1 = left
        a2, b2 = right
        return a1 * a2, a2 * b1 + b2

    _, h = lax.associative_scan(combine, (a, inp), axis=1)
    return h.astype(x.dtype)


def chunked_spatial_gating(u, v, w_s, b_s):
    B, L, _ = u.shape
    nc = L // CHUNK
    vg = v.reshape(B, nc, CHUNK, GMLP_GROUPS, GMLP_GROUP_DIM)
    mask = jnp.tril(jnp.ones((CHUNK, CHUNK), dtype=bool))
    w = jnp.where(mask[None], w_s, 0).astype(v.dtype)
    mixed = jnp.einsum('gts,bcsgd->bctgd', w, vg) + jnp.transpose(b_s).astype(v.dtype)[:, :, None]
    return u * mixed.reshape(B, L, GMLP_WIDTH)


def hybrid_layer(x, p_i, positions, g_pre_mix, w_in, conv_w, conv_b, w_rg_a, b_rg_a, w_rg_x, b_rg_x,
                 lru_lambda, g_gmlp_v, w_spatial, b_spatial, w_branch, w_out, g_post_mix,
                 g_pre_ffn, w_ffn_up, w_ffn_down, g_post_ffn, w_ple, w_ple_gate, g_post_ple):
    B, L, D = x.shape
    h = rms_norm(x, g_pre_mix)
    proj = h @ w_in
    offsets = [int(o) for o in np.cumsum(IN_SPLITS)[:-1]]
    q, k, v, qi, ki, wi, xr, gr, zu, zv, gate_logits = jnp.split(proj, offsets, axis=-1)

    q = apply_partial_rope(q.reshape(B, L, N_HEADS_ATTN, HEAD_DIM), positions, ROPE_DIM)
    k = apply_partial_rope(k.reshape(B, L, N_HEADS_ATTN, HEAD_DIM), positions, ROPE_DIM)
    v = v.reshape(B, L, N_HEADS_ATTN, HEAD_DIM)
    qi = apply_partial_rope(qi.reshape(B, L, IDX_HEADS, IDX_DIM), positions, IDX_ROPE_DIM)
    ki = apply_partial_rope(ki[:, :, None, :], positions, IDX_ROPE_DIM)[:, :, 0, :]
    y_a = dsa_attention(q, k, v, qi, ki, wi)

    xr = causal_depthwise_conv(xr, conv_w, conv_b)
    y_b = rg_lru(xr, w_rg_a, b_rg_a, w_rg_x, b_rg_x, lru_lambda) * jax.nn.gelu(gr)

    y_c = chunked_spatial_gating(jax.nn.gelu(zu), rms_norm(jax.nn.gelu(zv), g_gmlp_v), w_spatial, b_spatial)

    ys = jnp.stack([y_a, y_b, y_c], axis=0)
    branch = jnp.einsum('nbsw,nwd->nbsd', ys, w_branch)
    gates = jax.nn.sigmoid(gate_logits.reshape(B, L, N_BRANCH, D))
    merged = jnp.einsum('bsnd,nbsd->bsd', gates, branch)
    x = x + rms_norm(merged @ w_out, g_post_mix)

    h2 = rms_norm(x, g_pre_ffn)
    f = jnp.square(jax.nn.relu(h2 @ w_ffn_up)) @ w_ffn_down
    x = x + rms_norm(f, g_post_ffn)

    ple = (p_i.astype(x.dtype) @ w_ple) * jax.nn.sigmoid(x @ w_ple_gate)
    return x + rms_norm(ple, g_post_ple)


def setup_inputs(seed: int = 0) -> dict:
    key = jax.random.key(seed)
    ks = jax.random.split(key, 32)
    f32 = jnp.float32

    def nrm(k, shape, fan_in):
        return jax.random.normal(k, shape, f32) * (fan_in ** -0.5)

    def gain(k, n):
        return 1.0 + 0.05 * jax.random.normal(k, (DEPTH, n), f32)

    u = jax.random.uniform(ks[10], (DEPTH, LRU_WIDTH), f32, 0.9, 0.999)
    base = u ** (1.0 / LRU_C)
    lru_lambda = jnp.log(base) - jnp.log1p(-base)
    return {
        'x': jax.random.normal(ks[0], (BATCH, SEQ, D_MODEL), f32),
        'p': jax.random.normal(ks[1], (DEPTH, BATCH, SEQ, PLE_DIM), f32),
        'positions': jnp.broadcast_to(jnp.arange(SEQ, dtype=jnp.int32), (BATCH, SEQ)),
        'g_pre_mix': gain(ks[2], D_MODEL),
        'w_in': nrm(ks[3], (DEPTH, D_MODEL, IN_WIDTH), D_MODEL),
        'conv_w': nrm(ks[4], (DEPTH, CONV_WIDTH, LRU_WIDTH), CONV_WIDTH),
        'conv_b': 0.01 * jax.random.normal(ks[5], (DEPTH, LRU_WIDTH), f32),
        'w_rg_a': nrm(ks[6], (DEPTH, LRU_BLOCKS, LRU_BLOCK_DIM, LRU_BLOCK_DIM), LRU_BLOCK_DIM),
        'b_rg_a': 0.01 * jax.random.normal(ks[7], (DEPTH, LRU_WIDTH), f32),
        'w_rg_x': nrm(ks[8], (DEPTH, LRU_BLOCKS, LRU_BLOCK_DIM, LRU_BLOCK_DIM), LRU_BLOCK_DIM),
        'b_rg_x': 0.01 * jax.random.normal(ks[9], (DEPTH, LRU_WIDTH), f32),
        'lru_lambda': lru_lambda,
        'g_gmlp_v': gain(ks[11], GMLP_WIDTH),
        'w_spatial': nrm(ks[12], (DEPTH, GMLP_GROUPS, CHUNK, CHUNK), CHUNK),
        'b_spatial': 1.0 + 0.1 * jax.random.normal(ks[13], (DEPTH, GMLP_GROUPS, CHUNK), f32),
        'w_branch': nrm(ks[14], (DEPTH, N_BRANCH, BRANCH_WIDTH, D_MODEL), BRANCH_WIDTH),
        'w_out': nrm(ks[15], (DEPTH, D_MODEL, D_MODEL), D_MODEL),
        'g_post_mix': gain(ks[16], D_MODEL),
        'g_pre_ffn': gain(ks[17], D_MODEL),
        'w_ffn_up': nrm(ks[18], (DEPTH, D_MODEL, FFN_DIM), D_MODEL),
        'w_ffn_down': nrm(ks[19], (DEPTH, FFN_DIM, D_MODEL), FFN_DIM),
        'g_post_ffn': gain(ks[20], D_MODEL),
        'w_ple': nrm(ks[21], (DEPTH, PLE_DIM, D_MODEL), PLE_DIM),
        'w_ple_gate': nrm(ks[22], (DEPTH, D_MODEL, D_MODEL), D_MODEL),
        'g_post_ple': gain(ks[23], D_MODEL),
    }


def reference(x, p, positions, g_pre_mix, w_in, conv_w, conv_b, w_rg_a, b_rg_a, w_rg_x, b_rg_x,
              lru_lambda, g_gmlp_v, w_spatial, b_spatial, w_branch, w_out, g_post_mix,
              g_pre_ffn, w_ffn_up, w_ffn_down, g_post_ffn, w_ple, w_ple_gate, g_post_ple):
    for i in range(DEPTH):
        x = hybrid_layer(x, p[i], positions, g_pre_mix[i], w_in[i], conv_w[i], conv_b[i],
                         w_rg_a[i], b_rg_a[i], w_rg_x[i], b_rg_x[i], lru_lambda[i], g_gmlp_v[i],
                         w_spatial[i], b_spatial[i], w_branch[i], w_out[i], g_post_mix[i],
                         g_pre_ffn[i], w_ffn_up[i], w_ffn_down[i], g_post_ffn[i],
                         w_ple[i], w_ple_gate[i], g_post_ple[i])
    return x
```

```python
import functools

import jax
import jax.numpy as jnp
import numpy as np
from jax import lax
from jax.experimental import pallas as pl
from jax.experimental.pallas import tpu as pltpu

F32 = jnp.float32
BF16 = jnp.bfloat16
I32 = jnp.int32

N_HEADS = 8
HEAD_DIM = 64
ATTN_WIDTH = N_HEADS * HEAD_DIM
ROPE_DIM = HEAD_DIM // 4
ROPE_THETA = 500000.0
IDX_HEADS = 8
IDX_DIM = 64
INDEX_TOPK = 256
LRU_WIDTH = 512
LRU_BLOCKS = 8
CONV_WIDTH = 4
LRU_C = 8.0
GMLP_WIDTH = 512
GMLP_GROUPS = 8
GMLP_GROUP_DIM = GMLP_WIDTH // GMLP_GROUPS
CHUNK = 128
N_BRANCH = 3
EPS = 1e-6

LANES = 128
SUBLANES = 8
VMEM_LIMIT_BYTES = 56 * 1024 * 1024

INT_MIN = np.int32(-(2**31))
INT_MAX = np.int32(2**31 - 1)
MASK_BIAS = -1e30

PROJ_TOKENS = 512
DSA_ROWS = 128
DSA_KEYS = 512
LRU_TOKENS = 512
POST_TOKENS = 256
FFN_CHUNK = 1024

SEG_Q, SEG_K, SEG_V, SEG_QI = 0, 512, 1024, 1536
SEG_KIWI = 2048
SEG_XR, SEG_GR, SEG_ZU, SEG_ZV = 2176, 2688, 3200, 3712
SEG_GATE = 4224


def _rms(x, g):
    return x * lax.rsqrt(jnp.mean(x * x, axis=-1, keepdims=True) + EPS) * g


def _dot(a, b):
    return jnp.dot(a, b, preferred_element_type=F32)


def _params(n_grid):
    return pltpu.CompilerParams(
        dimension_semantics=("arbitrary",) * n_grid,
        vmem_limit_bytes=VMEM_LIMIT_BYTES)


def _const_spec(shape):
    zeros = (0,) * len(shape)
    return pl.BlockSpec(shape, lambda *_: zeros, pipeline_mode=pl.Buffered(1))


def _rope_table_kernel(pos_ref, invf_ref, c_ref, s_ref):
    ang = pos_ref[...].astype(F32) * invf_ref[...]
    lane = lax.broadcasted_iota(I32, ang.shape, 1) & (HEAD_DIM - 1)
    cos = jnp.cos(ang)
    sin = jnp.sin(ang)
    half = ROPE_DIM // 2
    c_ref[...] = jnp.where(lane < ROPE_DIM, cos, 1.0)
    s_ref[...] = jnp.where(lane < half, -sin, jnp.where(lane < ROPE_DIM, sin, 0.0))


def _rope_tables(positions):
    n = positions.size
    tile = 1024
    half = ROPE_DIM // 2
    inv_freq = ROPE_THETA ** (-jnp.arange(half, dtype=F32) * 2.0 / ROPE_DIM)
    inv_lane = jnp.tile(inv_freq, LANES // half).reshape(1, LANES)
    out = jax.ShapeDtypeStruct((n, LANES), F32)
    return pl.pallas_call(
        _rope_table_kernel,
        grid=(n // tile,),
        in_specs=[pl.BlockSpec((tile, 1), lambda i: (i, 0)),
                  pl.BlockSpec((1, LANES), lambda i: (0, 0))],
        out_specs=[pl.BlockSpec((tile, LANES), lambda i: (i, 0))] * 2,
        out_shape=[out, out],
        compiler_params=_params(1),
        name="rope_tables",
    )(positions.reshape(n, 1), inv_lane)


def _proj_kernel(x_ref, g_ref, w_ref, c_ref, s_ref, gv_ref, wsp_ref, bsp_ref,
                 q_ref, kt_ref, v_ref, qi_ref, kit_ref, wi_ref, xr_ref, gr_ref,
                 yc_ref, gate_ref):
    t = x_ref.shape[0]
    h = _rms(x_ref[...], g_ref[...]).astype(BF16)
    cos = c_ref[...]
    sin = s_ref[...]
    lane = lax.broadcasted_iota(I32, (t, LANES), 1)
    first_half = (lane & (HEAD_DIM - 1)) < (ROPE_DIM // 2)

    def proj(start, width):
        return _dot(h, w_ref[:, start:start + width])

    def rope(p):
        cols = []
        for j in range(p.shape[1] // LANES):
            pj = p[:, j * LANES:(j + 1) * LANES]
            partner = jnp.where(first_half,
                                pltpu.roll(pj, LANES - ROPE_DIM // 2, 1),
                                pltpu.roll(pj, ROPE_DIM // 2, 1))
            cols.append(pj * cos + partner * sin)
        return cols[0] if len(cols) == 1 else jnp.concatenate(cols, axis=1)

    q_ref[...] = (rope(proj(SEG_Q, ATTN_WIDTH)) * (HEAD_DIM ** -0.5)).astype(BF16)
    kt_ref[...] = rope(proj(SEG_K, ATTN_WIDTH)).T.astype(BF16)
    v_ref[...] = proj(SEG_V, ATTN_WIDTH).astype(BF16)
    qi_ref[...] = rope(proj(SEG_QI, IDX_HEADS * IDX_DIM)).astype(BF16)

    kiwi = proj(SEG_KIWI, LANES)
    kiwi = jnp.where(lane < IDX_DIM, rope(kiwi), kiwi)
    wi_ref[...] = kiwi
    kit_ref[...] = kiwi.T[0:IDX_DIM, :].astype(BF16)

    xr_ref[...] = proj(SEG_XR, LRU_WIDTH)
    gr_ref[...] = proj(SEG_GR, LRU_WIDTH)

    u = jax.nn.gelu(proj(SEG_ZU, GMLP_WIDTH))
    vn = _rms(jax.nn.gelu(proj(SEG_ZV, GMLP_WIDTH)), gv_ref[...]).astype(BF16)
    group = lax.broadcasted_iota(I32, (CHUNK, GMLP_WIDTH), 1) // GMLP_GROUP_DIM
    for c in range(t // CHUNK):
        vc = vn[c * CHUNK:(c + 1) * CHUNK, :]
        stack = jnp.concatenate(
            [jnp.where(group == g, vc, jnp.zeros_like(vc)) for g in range(GMLP_GROUPS)], axis=0)
        mixed = _dot(wsp_ref[...], stack) + bsp_ref[...]
        yc_ref[c * CHUNK:(c + 1) * CHUNK, :] = (u[c * CHUNK:(c + 1) * CHUNK, :] * mixed).astype(BF16)

    d = x_ref.shape[1]
    for n in range(N_BRANCH):
        gate_ref[:, n * d:(n + 1) * d] = jax.nn.sigmoid(proj(SEG_GATE + n * d, d)).astype(BF16)


def _proj_call(x, g_pre, w1, rope_c, rope_s, g_v, w_sp, b_sp):
    b, l, d = x.shape
    t = PROJ_TOKENS
    nt = l // t
    tok = lambda w: pl.BlockSpec((None, t, w), lambda bi, ti: (bi, ti, 0))
    tr = lambda w: pl.BlockSpec((None, w, t), lambda bi, ti: (bi, 0, ti))
    rope_spec = pl.BlockSpec((t, LANES), lambda bi, ti: (bi * nt + ti, 0))
    sds = lambda shape, dt: jax.ShapeDtypeStruct(shape, dt)
    return pl.pallas_call(
        _proj_kernel,
        grid=(b, nt),
        in_specs=[tok(d), _const_spec(g_pre.shape), _const_spec(w1.shape), rope_spec, rope_spec,
                  _const_spec(g_v.shape), _const_spec(w_sp.shape), _const_spec(b_sp.shape)],
        out_specs=[tok(ATTN_WIDTH), tr(ATTN_WIDTH), tok(ATTN_WIDTH), tok(IDX_HEADS * IDX_DIM),
                   tr(IDX_DIM), tok(LANES), tok(LRU_WIDTH), tok(LRU_WIDTH), tok(GMLP_WIDTH),
                   tok(N_BRANCH * d)],
        out_shape=[sds((b, l, ATTN_WIDTH), BF16), sds((b, ATTN_WIDTH, l), BF16),
                   sds((b, l, ATTN_WIDTH), BF16), sds((b, l, IDX_HEADS * IDX_DIM), BF16),
                   sds((b, IDX_DIM, l), BF16), sds((b, l, LANES), F32),
                   sds((b, l, LRU_WIDTH), F32), sds((b, l, LRU_WIDTH), F32),
                   sds((b, l, GMLP_WIDTH), BF16), sds((b, l, N_BRANCH * d), BF16)],
        compiler_params=_params(2),
        name="in_proj",
    )(x, g_pre, w1, rope_c, rope_s, g_v, w_sp, b_sp)


def _dsa_kernel(qi_ref, wi_ref, q_ref, kit_ref, kt_ref, v_ref, o_ref,
                key_ref, s_ref, j_ref, *, top_k):
    r = q_ref.shape[0]
    ck = DSA_KEYS
    ncol = ck // LANES
    row0 = pl.program_id(1) * r
    nk = (row0 + r + ck - 1) // ck
    idx_scale = (IDX_DIM ** -0.5) * (IDX_HEADS ** -0.5)

    def cols(a):
        return [a[:, c * LANES:(c + 1) * LANES] for c in range(ncol)]

    wi = wi_ref[...]
    w_rep = [jnp.broadcast_to(wi[:, IDX_DIM + h:IDX_DIM + h + 1], (r, LANES))
             for h in range(IDX_HEADS)]
    qpos = row0 + lax.broadcasted_iota(I32, (r, ck), 0)
    kcol = lax.broadcasted_iota(I32, (r, ck), 1)

    def score_chunk(kc, carry):
        k0 = pl.multiple_of(kc * ck, ck)
        kit = kit_ref[:, pl.ds(k0, ck)]
        acc = jnp.zeros((r, ck), F32)
        for h in range(IDX_HEADS):
            dots = _dot(qi_ref[:, h * IDX_DIM:(h + 1) * IDX_DIM], kit)
            acc = acc + jnp.concatenate([w_rep[h]] * ncol, axis=1) * jnp.maximum(dots, 0.0)
        bits = pltpu.bitcast(acc * idx_scale, I32)
        key = bits ^ ((bits >> 31) & INT_MAX)
        key_ref[:, pl.ds(k0, ck)] = jnp.where(k0 + kcol <= qpos, key, INT_MIN)
        return carry

    lax.fori_loop(0, nk, score_chunk, 0)

    def count(pred):
        def body(kc, cnt):
            k0 = pl.multiple_of(kc * ck, ck)
            for c, kcols in enumerate(cols(key_ref[:, pl.ds(k0, ck)])):
                cnt = cnt + jnp.where(pred(kcols, k0 + c * LANES), 1.0, 0.0)
            return cnt
        cnt = lax.fori_loop(0, nk, body, jnp.zeros((r, LANES), F32))
        return jnp.broadcast_to(jnp.sum(cnt, axis=1, keepdims=True), (r, LANES))

    def bit_step(b, lo):
        cand = lo + lax.shift_left(jnp.int32(1), 31 - b)
        n_ge = count(lambda k, _: k >= cand)
        return jnp.where(n_ge >= top_k, cand, lo)

    thr = lax.fori_loop(0, 32, bit_step, jnp.full((r, LANES), INT_MIN, I32))

    n_gt = count(lambda k, _: k > thr)
    n_eq = count(lambda k, _: k == thr)
    need = top_k - n_gt
    real = thr > INT_MIN
    has_tie = jnp.logical_and(real, n_eq > need)
    lane_pos = lax.broadcasted_iota(I32, (r, LANES), 1)
    j_ref[...] = jnp.where(real, INT_MAX, -1).astype(I32)

    @pl.when(jnp.max(jnp.where(has_tie, 1.0, 0.0)) > 0.0)
    def _():
        n_bits = max(1, int(kt_ref.shape[1] - 1).bit_length())

        def pos_step(b, last):
            cand = last + lax.shift_left(jnp.int32(1), n_bits - 1 - b)
            n_before = count(lambda k, p0: jnp.logical_and(k == thr, p0 + lane_pos < cand))
            return jnp.where(n_before < need, cand, last)

        last = lax.fori_loop(0, n_bits, pos_step, jnp.zeros((r, LANES), I32))
        j_ref[...] = jnp.where(has_tie, last, j_ref[...])

    last_tie = j_ref[...]

    def bias_chunk(kc, carry):
        k0 = pl.multiple_of(kc * ck, ck)
        out = []
        for c, kcols in enumerate(cols(key_ref[:, pl.ds(k0, ck)])):
            tie_ok = jnp.logical_and(kcols == thr, k0 + c * LANES + lane_pos <= last_tie)
            sel = jnp.logical_or(kcols > thr, tie_ok)
            out.append(jnp.where(sel, 0.0, MASK_BIAS).astype(F32))
        key_ref[:, pl.ds(k0, ck)] = pltpu.bitcast(jnp.concatenate(out, axis=1), I32)
        return carry

    lax.fori_loop(0, nk, bias_chunk, 0)

    low_lanes = lane_pos < HEAD_DIM
    for pair in range(N_HEADS // 2):
        lo = pair * LANES
        q2 = q_ref[:, lo:lo + LANES]
        res = []
        for sub in range(2):
            keep = low_lanes if sub == 0 else jnp.logical_not(low_lanes)
            qh = jnp.where(keep, q2, jnp.zeros_like(q2))

            def logits_chunk(kc, m, qh=qh):
                k0 = pl.multiple_of(kc * ck, ck)
                s = _dot(qh, kt_ref[lo:lo + LANES, pl.ds(k0, ck)])
                s = s + pltpu.bitcast(key_ref[:, pl.ds(k0, ck)], F32)
                s_ref[:, pl.ds(k0, ck)] = s
                for sc in cols(s):
                    m = jnp.maximum(m, sc)
                return m

            m = lax.fori_loop(0, nk, logits_chunk, jnp.full((r, LANES), -jnp.inf, F32))
            m = jnp.broadcast_to(jnp.max(m, axis=1, keepdims=True), (r, LANES))

            def pv_chunk(kc, carry, m=m):
                l, acc = carry
                k0 = pl.multiple_of(kc * ck, ck)
                p = [jnp.exp(sc - m) for sc in cols(s_ref[:, pl.ds(k0, ck)])]
                for pc in p:
                    l = l + pc
                pb = jnp.concatenate(p, axis=1).astype(BF16)
                acc = acc + _dot(pb, v_ref[pl.ds(k0, ck), lo:lo + LANES])
                return l, acc

            l, acc = lax.fori_loop(0, nk, pv_chunk,
                                   (jnp.zeros((r, LANES), F32), jnp.zeros((r, LANES), F32)))
            l = jnp.broadcast_to(jnp.sum(l, axis=1, keepdims=True), (r, LANES))
            res.append(acc / l)
        o_ref[:, lo:lo + LANES] = jnp.where(low_lanes, res[0], res[1]).astype(BF16)


def _dsa_call(qi, wi, q, kit, kt, v):
    b, l, _ = q.shape
    r = DSA_ROWS
    top_k = min(INDEX_TOPK, l // 4)
    row = lambda w: pl.BlockSpec((None, r, w), lambda bi, ti: (bi, ti, 0))
    full = lambda h, w: pl.BlockSpec((None, h, w), lambda bi, ti: (bi, 0, 0))
    return pl.pallas_call(
        functools.partial(_dsa_kernel, top_k=top_k),
        grid=(b, l // r),
        in_specs=[row(IDX_HEADS * IDX_DIM), row(LANES), row(ATTN_WIDTH),
                  full(IDX_DIM, l), full(ATTN_WIDTH, l), full(l, ATTN_WIDTH)],
        out_specs=row(ATTN_WIDTH),
        out_shape=jax.ShapeDtypeStruct((b, l, ATTN_WIDTH), BF16),
        scratch_shapes=[pltpu.VMEM((r, l), I32), pltpu.VMEM((r, l), F32),
                        pltpu.VMEM((r, LANES), I32)],
        compiler_params=_params(2),
        name="dsa_attention",
    )(qi, wi, q, kit, kt, v)


def _lru_kernel(xr_ref, gr_ref, cw_ref, cb_ref, wa_ref, ba_ref, wx_ref, bx_ref, lam_ref,
                y_ref, hist_ref, h_ref):
    t = xr_ref.shape[0]

    @pl.when(pl.program_id(1) == 0)
    def _():
        hist_ref[...] = jnp.zeros_like(hist_ref)
        h_ref[...] = jnp.zeros_like(h_ref)

    x = xr_ref[...]
    hist = hist_ref[...]
    row = lax.broadcasted_iota(I32, x.shape, 0)
    row8 = lax.broadcasted_iota(I32, hist.shape, 0)
    cw = cw_ref[...]
    conv = x * cw[CONV_WIDTH - 1:CONV_WIDTH, :] + cb_ref[...]
    for d in range(1, CONV_WIDTH):
        shifted = pltpu.roll(x, d, 0)
        head = jnp.where(row8 < d, pltpu.roll(hist, d, 0), shifted[0:SUBLANES, :])
        shifted = jnp.concatenate([head, shifted[SUBLANES:, :]], axis=0)
        conv = conv + shifted * cw[CONV_WIDTH - 1 - d:CONV_WIDTH - d, :]
    hist_ref[...] = x[t - SUBLANES:, :]

    cb16 = conv.astype(BF16)
    rg = jax.nn.sigmoid(_dot(cb16, wa_ref[...]) + ba_ref[...])
    ig = jax.nn.sigmoid(_dot(cb16, wx_ref[...]) + bx_ref[...])
    log_a = -LRU_C * rg * jax.nn.softplus(-lam_ref[...])
    a = jnp.exp(log_a)
    inp = jnp.sqrt(-jnp.tanh(log_a) * (a * a + 1.0)) * (ig * conv)

    d = 1
    while d < t:
        ok = row >= d
        inp = jnp.where(ok, a * pltpu.roll(inp, d, 0) + inp, inp)
        a = jnp.where(ok, a * pltpu.roll(a, d, 0), a)
        d *= 2
    hcur = inp + a * h_ref[0:1, :]
    h_ref[...] = jnp.broadcast_to(hcur[t - 1:t, :], h_ref.shape)
    y_ref[...] = (hcur * jax.nn.gelu(gr_ref[...])).astype(BF16)


def _lru_call(xr, gr, conv_w, conv_b, wa, ba, wx, bx, lam):
    b, l, w = xr.shape
    t = LRU_TOKENS
    tok = pl.BlockSpec((None, t, w), lambda bi, ti: (bi, ti, 0))
    consts = [conv_w, conv_b, wa, ba, wx, bx, lam]
    return pl.pallas_call(
        _lru_kernel,
        grid=(b, l // t),
        in_specs=[tok, tok] + [_const_spec(c.shape) for c in consts],
        out_specs=tok,
        out_shape=jax.ShapeDtypeStruct((b, l, w), BF16),
        scratch_shapes=[pltpu.VMEM((SUBLANES, w), F32), pltpu.VMEM((SUBLANES, w), F32)],
        compiler_params=_params(2),
        name="conv_rglru",
    )(xr, gr, *consts)


def _post_kernel(x_ref, ya_ref, yb_ref, yc_ref, gate_ref, p_ref,
                 wb_ref, wo_ref, gpm_ref, gpf_ref, wup_ref, wdn_ref, gpo_ref,
                 wple_ref, wpg_ref, gpp_ref, o_ref):
    d = x_ref.shape[1]
    w = ya_ref.shape[1]
    x = x_ref[...]
    merged = jnp.zeros(x.shape, F32)
    for n, y_ref in enumerate((ya_ref, yb_ref, yc_ref)):
        branch = _dot(y_ref[...], wb_ref[n * w:(n + 1) * w, :])
        merged = merged + gate_ref[:, n * d:(n + 1) * d].astype(F32) * branch
    x = x + _rms(_dot(merged.astype(BF16), wo_ref[...]), gpm_ref[...])

    h2 = _rms(x, gpf_ref[...]).astype(BF16)
    f = jnp.zeros(x.shape, F32)
    for c in range(wup_ref.shape[1] // FFN_CHUNK):
        sl = slice(c * FFN_CHUNK, (c + 1) * FFN_CHUNK)
        hid = jnp.square(jnp.maximum(_dot(h2, wup_ref[:, sl]), 0.0)).astype(BF16)
        f = f + _dot(hid, wdn_ref[sl, :])
    x = x + _rms(f, gpo_ref[...])

    ple = _dot(p_ref[...].astype(BF16), wple_ref[...]) * jax.nn.sigmoid(
        _dot(x.astype(BF16), wpg_ref[...]))
    o_ref[...] = x + _rms(ple, gpp_ref[...])


def _post_call(x, ya, yb, yc, gates, p, consts):
    b, l, d = x.shape
    t = POST_TOKENS
    tok = lambda w: pl.BlockSpec((None, t, w), lambda bi, ti: (bi, ti, 0))
    return pl.pallas_call(
        _post_kernel,
        grid=(b, l // t),
        in_specs=[tok(d), tok(ya.shape[2]), tok(yb.shape[2]), tok(yc.shape[2]),
                  tok(gates.shape[2]), tok(p.shape[2])] + [_const_spec(c.shape) for c in consts],
        out_specs=tok(d),
        out_shape=jax.ShapeDtypeStruct((b, l, d), F32),
        compiler_params=_params(2),
        name="merge_ffn_ple",
    )(x, ya, yb, yc, gates, p, *consts)


def _block_diag(w):
    nb, bi, bo = w.shape
    eye = jnp.eye(nb, dtype=w.dtype)
    return jnp.einsum("hij,hg->higj", w, eye).reshape(nb * bi, nb * bo)


def _pack_w_in(w_in):
    d = w_in.shape[0]
    kiwi = w_in[:, SEG_KIWI:SEG_KIWI + IDX_DIM + IDX_HEADS]
    pad = jnp.zeros((d, LANES - IDX_DIM - IDX_HEADS), w_in.dtype)
    rest = w_in[:, SEG_KIWI + IDX_DIM + IDX_HEADS:]
    return jnp.concatenate([w_in[:, :SEG_KIWI], kiwi, pad, rest], axis=1).astype(BF16)


def kernel(x, p, positions, g_pre_mix, w_in, conv_w, conv_b, w_rg_a, b_rg_a, w_rg_x, b_rg_x,
           lru_lambda, g_gmlp_v, w_spatial, b_spatial, w_branch, w_out, g_post_mix,
           g_pre_ffn, w_ffn_up, w_ffn_down, g_post_ffn, w_ple, w_ple_gate, g_post_ple):
    depth = w_in.shape[0]
    b, l, d = x.shape
    assert l % PROJ_TOKENS == 0 and l % DSA_KEYS == 0 and l % LRU_TOKENS == 0
    assert l % POST_TOKENS == 0 and (b * l) % 1024 == 0 and DSA_KEYS >= INDEX_TOPK
    row = lambda a: a.reshape(1, -1)

    rope_c, rope_s = _rope_tables(positions)
    tril = jnp.tril(jnp.ones((CHUNK, CHUNK), dtype=bool))

    for i in range(depth):
        w_sp = jnp.where(tril[None], w_spatial[i], 0)
        w_sp = jnp.transpose(w_sp, (1, 0, 2)).reshape(CHUNK, GMLP_GROUPS * CHUNK).astype(BF16)
        b_sp = jnp.repeat(jnp.transpose(b_spatial[i]), GMLP_GROUP_DIM, axis=1)

        q, kt, v, qi, kit, wi, xr, gr, yc, gates = _proj_call(
            x, row(g_pre_mix[i]), _pack_w_in(w_in[i]), rope_c, rope_s,
            row(g_gmlp_v[i]), w_sp, b_sp)

        ya = _dsa_call(qi, wi, q, kit, kt, v)

        yb = _lru_call(xr, gr, conv_w[i], row(conv_b[i]),
                       _block_diag(w_rg_a[i]).astype(BF16), row(b_rg_a[i]),
                       _block_diag(w_rg_x[i]).astype(BF16), row(b_rg_x[i]),
                       row(lru_lambda[i]))

        consts = [w_branch[i].reshape(N_BRANCH * w_branch.shape[2], d).astype(BF16),
                  w_out[i].astype(BF16), row(g_post_mix[i]), row(g_pre_ffn[i]),
                  w_ffn_up[i].astype(BF16), w_ffn_down[i].astype(BF16), row(g_post_ffn[i]),
                  w_ple[i].astype(BF16), w_ple_gate[i].astype(BF16), row(g_post_ple[i])]
        x = _post_call(x, ya, yb, yc, gates, p[i], consts)
    return x
```

```python
import functools

import jax
import jax.numpy as jnp
import numpy as np
from jax import lax
from jax.experimental import pallas as pl
from jax.experimental.pallas import tpu as pltpu

F32 = jnp.float32
BF16 = jnp.bfloat16
I32 = jnp.int32
I16 = jnp.int16

N_HEADS = 8
HEAD_DIM = 64
ATTN_WIDTH = N_HEADS * HEAD_DIM
ROPE_DIM = HEAD_DIM // 4
ROPE_THETA = 500000.0
IDX_HEADS = 8
IDX_DIM = 64
INDEX_TOPK = 256
LRU_WIDTH = 512
LRU_BLOCKS = 8
CONV_WIDTH = 4
LRU_C = 8.0
GMLP_WIDTH = 512
GMLP_GROUPS = 8
GMLP_GROUP_DIM = GMLP_WIDTH // GMLP_GROUPS
CHUNK = 128
N_BRANCH = 3
EPS = 1e-6

LANES = 128
SUBLANES = 8
VMEM_LIMIT_BYTES = 56 * 1024 * 1024

INT_MIN = np.int32(-(2**31))
INT_MAX = np.int32(2**31 - 1)
MASK_BIAS = -1e30
Q_SCALE = float(np.log2(np.e)) * HEAD_DIM ** -0.5

PROJ_TOKENS = 512
DSA_ROWS = 256
DSA_KEYS = 512
DSA_SUB = 128
DSA_GROUP = 4
LRU_TOKENS = 512
POST_TOKENS = 256
FFN_CHUNK = 1024

SEG_Q, SEG_K, SEG_V, SEG_QI = 0, 512, 1024, 1536
SEG_KIWI = 2048
SEG_XR, SEG_GR, SEG_ZU, SEG_ZV = 2176, 2688, 3200, 3712
SEG_GATE = 4224


def _rms(x, g):
    return x * lax.rsqrt(jnp.mean(x * x, axis=-1, keepdims=True) + EPS) * g


def _dot(a, b):
    return jnp.dot(a, b, preferred_element_type=F32)


def _params(n_grid):
    return pltpu.CompilerParams(
        dimension_semantics=("arbitrary",) * n_grid,
        vmem_limit_bytes=VMEM_LIMIT_BYTES)


def _const_spec(shape):
    zeros = (0,) * len(shape)
    return pl.BlockSpec(shape, lambda *_: zeros, pipeline_mode=pl.Buffered(1))


def _rope_table_kernel(pos_ref, invf_ref, c_ref, s_ref):
    ang = pos_ref[...].astype(F32) * invf_ref[...]
    lane = lax.broadcasted_iota(I32, ang.shape, 1) & (HEAD_DIM - 1)
    cos = jnp.cos(ang)
    sin = jnp.sin(ang)
    half = ROPE_DIM // 2
    c_ref[...] = jnp.where(lane < ROPE_DIM, cos, 1.0)
    s_ref[...] = jnp.where(lane < half, -sin, jnp.where(lane < ROPE_DIM, sin, 0.0))


def _rope_tables(positions):
    n = positions.size
    tile = 1024
    half = ROPE_DIM // 2
    inv_freq = ROPE_THETA ** (-jnp.arange(half, dtype=F32) * 2.0 / ROPE_DIM)
    inv_lane = jnp.tile(inv_freq, LANES // half).reshape(1, LANES)
    out = jax.ShapeDtypeStruct((n, LANES), F32)
    return pl.pallas_call(
        _rope_table_kernel,
        grid=(n // tile,),
        in_specs=[pl.BlockSpec((tile, 1), lambda i: (i, 0)),
                  pl.BlockSpec((1, LANES), lambda i: (0, 0))],
        out_specs=[pl.BlockSpec((tile, LANES), lambda i: (i, 0))] * 2,
        out_shape=[out, out],
        compiler_params=_params(1),
        name="rope_tables",
    )(positions.reshape(n, 1), inv_lane)


def _proj_kernel(x_ref, g_ref, w_ref, c_ref, s_ref, gv_ref, wsp_ref, bsp_ref,
                 qt_ref, k_ref, vt_ref, qit_ref, ki_ref, wit_ref, xr_ref, gr_ref,
                 yc_ref, gate_ref):
    t = x_ref.shape[0]
    h = _rms(x_ref[...], g_ref[...]).astype(BF16)
    cos = c_ref[...]
    sin = s_ref[...]
    lane = lax.broadcasted_iota(I32, (t, LANES), 1)
    first_half = (lane & (HEAD_DIM - 1)) < (ROPE_DIM // 2)

    def proj(start, width):
        return _dot(h, w_ref[:, start:start + width])

    def rope(p):
        cols = []
        for j in range(p.shape[1] // LANES):
            pj = p[:, j * LANES:(j + 1) * LANES]
            partner = jnp.where(first_half,
                                pltpu.roll(pj, LANES - ROPE_DIM // 2, 1),
                                pltpu.roll(pj, ROPE_DIM // 2, 1))
            cols.append(pj * cos + partner * sin)
        return cols[0] if len(cols) == 1 else jnp.concatenate(cols, axis=1)

    qt_ref[...] = (rope(proj(SEG_Q, ATTN_WIDTH)) * Q_SCALE).T.astype(BF16)
    k_ref[...] = rope(proj(SEG_K, ATTN_WIDTH)).astype(BF16)
    vt_ref[...] = proj(SEG_V, ATTN_WIDTH).T.astype(BF16)
    qit_ref[...] = rope(proj(SEG_QI, IDX_HEADS * IDX_DIM)).T.astype(BF16)

    kiwi = proj(SEG_KIWI, LANES)
    kiwi = jnp.where(lane < IDX_DIM, rope(kiwi), kiwi)
    ki_ref[...] = jnp.where(lane < IDX_DIM, kiwi, pltpu.roll(kiwi, IDX_DIM, 1)).astype(BF16)
    wit_ref[...] = kiwi.T[IDX_DIM:IDX_DIM + IDX_HEADS, :]

    xr_ref[...] = proj(SEG_XR, LRU_WIDTH)
    gr_ref[...] = proj(SEG_GR, LRU_WIDTH)

    u = jax.nn.gelu(proj(SEG_ZU, GMLP_WIDTH))
    vn = _rms(jax.nn.gelu(proj(SEG_ZV, GMLP_WIDTH)), gv_ref[...]).astype(BF16)
    group = lax.broadcasted_iota(I32, (CHUNK, GMLP_WIDTH), 1) // GMLP_GROUP_DIM
    for c in range(t // CHUNK):
        vc = vn[c * CHUNK:(c + 1) * CHUNK, :]
        stack = jnp.concatenate(
            [jnp.where(group == g, vc, jnp.zeros_like(vc)) for g in range(GMLP_GROUPS)], axis=0)
        mixed = _dot(wsp_ref[...], stack) + bsp_ref[...]
        yc_ref[c * CHUNK:(c + 1) * CHUNK, :] = (u[c * CHUNK:(c + 1) * CHUNK, :] * mixed).astype(BF16)

    d = x_ref.shape[1]
    for n in range(N_BRANCH):
        gate_ref[:, n * d:(n + 1) * d] = jax.nn.sigmoid(proj(SEG_GATE + n * d, d)).astype(BF16)


def _proj_call(x, g_pre, w1, rope_c, rope_s, g_v, w_sp, b_sp):
    b, l, d = x.shape
    t = PROJ_TOKENS
    nt = l // t
    tok = lambda w: pl.BlockSpec((None, t, w), lambda bi, ti: (bi, ti, 0))
    tr = lambda w: pl.BlockSpec((None, w, t), lambda bi, ti: (bi, 0, ti))
    rope_spec = pl.BlockSpec((t, LANES), lambda bi, ti: (bi * nt + ti, 0))
    sds = lambda shape, dt: jax.ShapeDtypeStruct(shape, dt)
    return pl.pallas_call(
        _proj_kernel,
        grid=(b, nt),
        in_specs=[tok(d), _const_spec(g_pre.shape), _const_spec(w1.shape), rope_spec, rope_spec,
                  _const_spec(g_v.shape), _const_spec(w_sp.shape), _const_spec(b_sp.shape)],
        out_specs=[tr(ATTN_WIDTH), tok(ATTN_WIDTH), tr(ATTN_WIDTH), tr(IDX_HEADS * IDX_DIM),
                   tok(LANES), tr(IDX_HEADS), tok(LRU_WIDTH), tok(LRU_WIDTH), tok(GMLP_WIDTH),
                   tok(N_BRANCH * d)],
        out_shape=[sds((b, ATTN_WIDTH, l), BF16), sds((b, l, ATTN_WIDTH), BF16),
                   sds((b, ATTN_WIDTH, l), BF16), sds((b, IDX_HEADS * IDX_DIM, l), BF16),
                   sds((b, l, LANES), BF16), sds((b, IDX_HEADS, l), F32),
                   sds((b, l, LRU_WIDTH), F32), sds((b, l, LRU_WIDTH), F32),
                   sds((b, l, GMLP_WIDTH), BF16), sds((b, l, N_BRANCH * d), BF16)],
        compiler_params=_params(2),
        name="in_proj",
    )(x, g_pre, w1, rope_c, rope_s, g_v, w_sp, b_sp)


def _dsa_kernel(qit_ref, wit_ref, qt_ref, ki_ref, k_ref, vt_ref, o_ref,
                key_ref, hi_ref, lo_ref, s_ref, ot_ref, *, top_k):
    r = qt_ref.shape[1]
    l_keys = k_ref.shape[0]
    ck = DSA_KEYS
    sub = DSA_SUB
    nsub = ck // sub
    col0 = pl.program_id(1) * r
    nk = (col0 + r + ck - 1) // ck
    idx_scale = (IDX_DIM ** -0.5) * (IDX_HEADS ** -0.5)

    slab_row = lax.broadcasted_iota(I32, (LANES, r), 0)

    def head_weights(ref, h):
        slab = ref[(h // 2) * LANES:(h // 2 + 1) * LANES, :]
        keep = (slab_row < HEAD_DIM) if h % 2 == 0 else (slab_row >= HEAD_DIM)
        return jnp.where(keep, slab, jnp.zeros_like(slab))

    def chunk_loop(body, init):
        def wrapped(kc, carry):
            return body(pl.multiple_of(kc * ck, ck), carry)
        return lax.fori_loop(0, nk, wrapped, init)

    wit = wit_ref[...]
    w_idx = [head_weights(qit_ref, h) for h in range(IDX_HEADS)]
    qpos = col0 + lax.broadcasted_iota(I32, (sub, r), 1)
    krow = lax.broadcasted_iota(I32, (sub, r), 0)

    def score_chunk(k0, carry):
        for sb in range(nsub):
            k1 = k0 + sb * sub
            ki = ki_ref[pl.ds(k1, sub), :]
            acc = jnp.zeros((sub, r), F32)
            for h in range(IDX_HEADS):
                acc = acc + wit[h:h + 1, :] * jnp.maximum(_dot(ki, w_idx[h]), 0.0)
            bits = pltpu.bitcast(acc * idx_scale, I32)
            key = bits ^ ((bits >> 31) & INT_MAX)
            key = jnp.where(k1 + krow <= qpos, key, INT_MIN)
            key_ref[pl.ds(k1, sub), :] = key
            hi_ref[pl.ds(k1, sub), :] = (key >> 16).astype(I16)
            lo_ref[pl.ds(k1, sub), :] = ((key & 0xFFFF) - 32768).astype(I16)
        return carry

    chunk_loop(score_chunk, 0)

    n_acc = 4

    def count(ref, rows, pred):
        dt = ref.dtype
        one, zero = jnp.ones((), dt), jnp.zeros((), dt)

        def body(k0, accs):
            accs = list(accs)
            chunk = ref[pl.ds(k0, ck), :]
            for j in range(ck // rows):
                blk = chunk[j * rows:(j + 1) * rows, :]
                accs[j % n_acc] = accs[j % n_acc] + jnp.where(pred(blk, k0 + j * rows), one, zero)
            return tuple(accs)

        accs = chunk_loop(body, tuple(jnp.zeros((rows, r), dt) for _ in range(n_acc)))
        tot = accs[0].astype(I32)
        for a in accs[1:]:
            tot = tot + a.astype(I32)
        return jnp.sum(tot, axis=0, keepdims=True)

    def bisect16(ref, need):
        rows = 2 * SUBLANES

        def step(b, lo):
            cand = lo + lax.shift_left(jnp.int32(1), 15 - b)
            cand16 = jnp.broadcast_to(cand.astype(I16), (rows, r))
            n_ge = count(ref, rows, lambda blk, _: blk >= cand16)
            return jnp.where(n_ge >= need, cand, lo)

        return lax.fori_loop(0, 16, step, jnp.full((1, r), -32768, I32))

    thr_hi = bisect16(hi_ref, top_k)
    thr_hi16 = jnp.broadcast_to(thr_hi.astype(I16), (2 * SUBLANES, r))
    n_hi_gt = count(hi_ref, 2 * SUBLANES, lambda blk, _: blk > thr_hi16)

    def mask_low(k0, carry):
        sl = pl.ds(k0, ck)
        match = hi_ref[sl, :] == jnp.broadcast_to(thr_hi.astype(I16), (ck, r))
        lo_ref[sl, :] = jnp.where(match, lo_ref[sl, :], jnp.int16(-32768))
        return carry

    chunk_loop(mask_low, 0)
    thr_lo = bisect16(lo_ref, top_k - n_hi_gt)
    thr = thr_hi * 65536 + (thr_lo + 32768)

    thr8 = jnp.broadcast_to(thr, (SUBLANES, r))
    n_gt = count(key_ref, SUBLANES, lambda blk, _: blk > thr8)
    n_eq = count(key_ref, SUBLANES, lambda blk, _: blk == thr8)
    need = top_k - n_gt
    real = thr > INT_MIN
    has_tie = jnp.logical_and(real, n_eq > need)
    row8 = lax.broadcasted_iota(I32, (SUBLANES, r), 0)

    def tie_search():
        n_bits = max(1, int(l_keys - 1).bit_length())

        def pos_step(b, last):
            cand = last + lax.shift_left(jnp.int32(1), n_bits - 1 - b)
            n_before = count(key_ref, SUBLANES,
                             lambda blk, p0: jnp.logical_and(blk == thr8, p0 + row8 < cand))
            return jnp.where(n_before < need, cand, last)

        return lax.fori_loop(0, n_bits, pos_step, jnp.zeros((1, r), I32))

    any_tie = jnp.max(jnp.where(has_tie, 1.0, 0.0)) > 0.0
    key_row = lax.broadcasted_iota(I32, (ck, r), 0)

    def write_bias(bias_of):
        def bias_chunk(k0, carry):
            sl = pl.ds(k0, ck)
            key_ref[sl, :] = pltpu.bitcast(bias_of(key_ref[sl, :], k0), I32)
            return carry
        chunk_loop(bias_chunk, 0)

    def keep(selected):
        return jnp.where(selected, 0.0, MASK_BIAS).astype(F32)

    @pl.when(jnp.logical_not(any_tie))
    def _():
        thr_b = jnp.broadcast_to(jnp.where(real, thr, INT_MIN + 1), (ck, r))
        write_bias(lambda key, k0: keep(key >= thr_b))

    @pl.when(any_tie)
    def _():
        last = jnp.where(has_tie, tie_search(), jnp.where(real, INT_MAX, -1))
        thr_b = jnp.broadcast_to(thr, (ck, r))
        last_b = jnp.broadcast_to(last, (ck, r))
        write_bias(lambda key, k0: jnp.where(key == thr_b, keep(k0 + key_row <= last_b),
                                             keep(key > thr_b)))

    def fold8(a, op):
        out = a[0:SUBLANES, :]
        for j in range(1, a.shape[0] // SUBLANES):
            out = op(out, a[j * SUBLANES:(j + 1) * SUBLANES, :])
        return out

    for g0 in range(0, N_HEADS, DSA_GROUP):
        heads = list(range(g0, g0 + DSA_GROUP))
        w_q = [head_weights(qt_ref, h) for h in heads]

        def logits_chunk(k0, ms):
            ms = list(ms)
            for sb in range(nsub):
                k1 = k0 + sb * sub
                bias = pltpu.bitcast(key_ref[pl.ds(k1, sub), :], F32)
                for i, h in enumerate(heads):
                    pair = (h // 2) * LANES
                    s = _dot(k_ref[pl.ds(k1, sub), pair:pair + LANES], w_q[i]) + bias
                    s_ref[i, pl.ds(k1, sub), :] = s
                    ms[i] = jnp.maximum(ms[i], fold8(s, jnp.maximum))
            return tuple(ms)

        ms = chunk_loop(logits_chunk,
                        tuple(jnp.full((SUBLANES, r), -jnp.inf, F32) for _ in heads))
        ms = [jnp.broadcast_to(jnp.max(m, axis=0, keepdims=True), (sub, r)) for m in ms]

        for i, h in enumerate(heads):
            ot_ref[h * HEAD_DIM:(h + 1) * HEAD_DIM, :] = jnp.zeros((HEAD_DIM, r), F32)

        def pv_chunk(k0, ls):
            ls = list(ls)
            for i, h in enumerate(heads):
                acc = jnp.zeros((HEAD_DIM, r), F32)
                for sb in range(nsub):
                    k1 = k0 + sb * sub
                    p = jnp.exp2(s_ref[i, pl.ds(k1, sub), :] - ms[i])
                    ls[i] = ls[i] + fold8(p, jnp.add)
                    acc = acc + _dot(vt_ref[h * HEAD_DIM:(h + 1) * HEAD_DIM, pl.ds(k1, sub)],
                                     p.astype(BF16))
                ot_ref[h * HEAD_DIM:(h + 1) * HEAD_DIM, :] += acc
            return tuple(ls)

        ls = chunk_loop(pv_chunk, tuple(jnp.zeros((SUBLANES, r), F32) for _ in heads))
        for i, h in enumerate(heads):
            inv = 1.0 / jnp.sum(ls[i], axis=0, keepdims=True)
            ot_ref[h * HEAD_DIM:(h + 1) * HEAD_DIM, :] = ot_ref[h * HEAD_DIM:(h + 1) * HEAD_DIM, :] * inv

    o_ref[...] = ot_ref[...].T.astype(BF16)


def _dsa_call(qit, wit, qt, ki, k, vt):
    b, l, _ = k.shape
    r = DSA_ROWS
    top_k = min(INDEX_TOPK, l // 4)
    col = lambda h: pl.BlockSpec((None, h, r), lambda bi, ti: (bi, 0, ti))
    full = lambda h, w: pl.BlockSpec((None, h, w), lambda bi, ti: (bi, 0, 0),
                                     pipeline_mode=pl.Buffered(1))
    return pl.pallas_call(
        functools.partial(_dsa_kernel, top_k=top_k),
        grid=(b, l // r),
        in_specs=[col(IDX_HEADS * IDX_DIM), col(IDX_HEADS), col(ATTN_WIDTH),
                  full(l, LANES), full(l, ATTN_WIDTH), full(ATTN_WIDTH, l)],
        out_specs=pl.BlockSpec((None, r, ATTN_WIDTH), lambda bi, ti: (bi, ti, 0)),
        out_shape=jax.ShapeDtypeStruct((b, l, ATTN_WIDTH), BF16),
        scratch_shapes=[pltpu.VMEM((l, r), I32), pltpu.VMEM((l, r), I16), pltpu.VMEM((l, r), I16),
                        pltpu.VMEM((DSA_GROUP, l, r), F32), pltpu.VMEM((ATTN_WIDTH, r), F32)],
        compiler_params=_params(2),
        name="dsa_attention",
    )(qit, wit, qt, ki, k, vt)


def _lru_kernel(xr_ref, gr_ref, cw_ref, cb_ref, wa_ref, ba_ref, wx_ref, bx_ref, lam_ref,
                y_ref, hist_ref, h_ref):
    t = xr_ref.shape[0]

    @pl.when(pl.program_id(1) == 0)
    def _():
        hist_ref[...] = jnp.zeros_like(hist_ref)
        h_ref[...] = jnp.zeros_like(h_ref)

    x = xr_ref[...]
    hist = hist_ref[...]
    row = lax.broadcasted_iota(I32, x.shape, 0)
    row8 = lax.broadcasted_iota(I32, hist.shape, 0)
    cw = cw_ref[...]
    conv = x * cw[CONV_WIDTH - 1:CONV_WIDTH, :] + cb_ref[...]
    for d in range(1, CONV_WIDTH):
        shifted = pltpu.roll(x, d, 0)
        head = jnp.where(row8 < d, pltpu.roll(hist, d, 0), shifted[0:SUBLANES, :])
        shifted = jnp.concatenate([head, shifted[SUBLANES:, :]], axis=0)
        conv = conv + shifted * cw[CONV_WIDTH - 1 - d:CONV_WIDTH - d, :]
    hist_ref[...] = x[t - SUBLANES:, :]

    cb16 = conv.astype(BF16)
    rg = jax.nn.sigmoid(_dot(cb16, wa_ref[...]) + ba_ref[...])
    ig = jax.nn.sigmoid(_dot(cb16, wx_ref[...]) + bx_ref[...])
    log_a = -LRU_C * rg * jax.nn.softplus(-lam_ref[...])
    a = jnp.exp(log_a)
    inp = jnp.sqrt(-jnp.tanh(log_a) * (a * a + 1.0)) * (ig * conv)

    d = 1
    while d < t:
        ok = row >= d
        inp = jnp.where(ok, a * pltpu.roll(inp, d, 0) + inp, inp)
        a = jnp.where(ok, a * pltpu.roll(a, d, 0), a)
        d *= 2
    hcur = inp + a * h_ref[0:1, :]
    h_ref[...] = jnp.broadcast_to(hcur[t - 1:t, :], h_ref.shape)
    y_ref[...] = (hcur * jax.nn.gelu(gr_ref[...])).astype(BF16)


def _lru_call(xr, gr, conv_w, conv_b, wa, ba, wx, bx, lam):
    b, l, w = xr.shape
    t = LRU_TOKENS
    tok = pl.BlockSpec((None, t, w), lambda bi, ti: (bi, ti, 0))
    consts = [conv_w, conv_b, wa, ba, wx, bx, lam]
    return pl.pallas_call(
        _lru_kernel,
        grid=(b, l // t),
        in_specs=[tok, tok] + [_const_spec(c.shape) for c in consts],
        out_specs=tok,
        out_shape=jax.ShapeDtypeStruct((b, l, w), BF16),
        scratch_shapes=[pltpu.VMEM((SUBLANES, w), F32), pltpu.VMEM((SUBLANES, w), F32)],
        compiler_params=_params(2),
        name="conv_rglru",
    )(xr, gr, *consts)


def _post_kernel(x_ref, ya_ref, yb_ref, yc_ref, gate_ref, p_ref,
                 wb_ref, wo_ref, gpm_ref, gpf_ref, wup_ref, wdn_ref, gpo_ref,
                 wple_ref, wpg_ref, gpp_ref, o_ref):
    d = x_ref.shape[1]
    w = ya_ref.shape[1]
    x = x_ref[...]
    merged = jnp.zeros(x.shape, F32)
    for n, y_ref in enumerate((ya_ref, yb_ref, yc_ref)):
        branch = _dot(y_ref[...], wb_ref[n * w:(n + 1) * w, :])
        merged = merged + gate_ref[:, n * d:(n + 1) * d].astype(F32) * branch
    x = x + _rms(_dot(merged.astype(BF16), wo_ref[...]), gpm_ref[...])

    h2 = _rms(x, gpf_ref[...]).astype(BF16)
    f = jnp.zeros(x.shape, F32)
    for c in range(wup_ref.shape[1] // FFN_CHUNK):
        sl = slice(c * FFN_CHUNK, (c + 1) * FFN_CHUNK)
        hid = jnp.square(jnp.maximum(_dot(h2, wup_ref[:, sl]), 0.0)).astype(BF16)
        f = f + _dot(hid, wdn_ref[sl, :])
    x = x + _rms(f, gpo_ref[...])

    ple = _dot(p_ref[...].astype(BF16), wple_ref[...]) * jax.nn.sigmoid(
        _dot(x.astype(BF16), wpg_ref[...]))
    o_ref[...] = x + _rms(ple, gpp_ref[...])


def _post_call(x, ya, yb, yc, gates, p, consts):
    b, l, d = x.shape
    t = POST_TOKENS
    tok = lambda w: pl.BlockSpec((None, t, w), lambda bi, ti: (bi, ti, 0))
    return pl.pallas_call(
        _post_kernel,
        grid=(b, l // t),
        in_specs=[tok(d), tok(ya.shape[2]), tok(yb.shape[2]), tok(yc.shape[2]),
                  tok(gates.shape[2]), tok(p.shape[2])] + [_const_spec(c.shape) for c in consts],
        out_specs=tok(d),
        out_shape=jax.ShapeDtypeStruct((b, l, d), F32),
        compiler_params=_params(2),
        name="merge_ffn_ple",
    )(x, ya, yb, yc, gates, p, *consts)


def _block_diag(w):
    nb, bi, bo = w.shape
    eye = jnp.eye(nb, dtype=w.dtype)
    return jnp.einsum("hij,hg->higj", w, eye).reshape(nb * bi, nb * bo)


def _pack_w_in(w_in):
    d = w_in.shape[0]
    kiwi = w_in[:, SEG_KIWI:SEG_KIWI + IDX_DIM + IDX_HEADS]
    pad = jnp.zeros((d, LANES - IDX_DIM - IDX_HEADS), w_in.dtype)
    rest = w_in[:, SEG_KIWI + IDX_DIM + IDX_HEADS:]
    return jnp.concatenate([w_in[:, :SEG_KIWI], kiwi, pad, rest], axis=1).astype(BF16)


def kernel(x, p, positions, g_pre_mix, w_in, conv_w, conv_b, w_rg_a, b_rg_a, w_rg_x, b_rg_x,
           lru_lambda, g_gmlp_v, w_spatial, b_spatial, w_branch, w_out, g_post_mix,
           g_pre_ffn, w_ffn_up, w_ffn_down, g_post_ffn, w_ple, w_ple_gate, g_post_ple):
    depth = w_in.shape[0]
    b, l, d = x.shape
    assert l % PROJ_TOKENS == 0 and l % DSA_KEYS == 0 and l % LRU_TOKENS == 0 and l % DSA_ROWS == 0
    assert l % POST_TOKENS == 0 and (b * l) % 1024 == 0 and DSA_KEYS >= INDEX_TOPK
    row = lambda a: a.reshape(1, -1)

    rope_c, rope_s = _rope_tables(positions)
    tril = jnp.tril(jnp.ones((CHUNK, CHUNK), dtype=bool))

    for i in range(depth):
        w_sp = jnp.where(tril[None], w_spatial[i], 0)
        w_sp = jnp.transpose(w_sp, (1, 0, 2)).reshape(CHUNK, GMLP_GROUPS * CHUNK).astype(BF16)
        b_sp = jnp.repeat(jnp.transpose(b_spatial[i]), GMLP_GROUP_DIM, axis=1)

        qt, k, vt, qit, ki, wit, xr, gr, yc, gates = _proj_call(
            x, row(g_pre_mix[i]), _pack_w_in(w_in[i]), rope_c, rope_s,
            row(g_gmlp_v[i]), w_sp, b_sp)

        ya = _dsa_call(qit, wit, qt, ki, k, vt)

        yb = _lru_call(xr, gr, conv_w[i], row(conv_b[i]),
                       _block_diag(w_rg_a[i]).astype(BF16), row(b_rg_a[i]),
                       _block_diag(w_rg_x[i]).astype(BF16), row(b_rg_x[i]),
                       row(lru_lambda[i]))

        consts = [w_branch[i].reshape(N_BRANCH * w_branch.shape[2], d).astype(BF16),
                  w_out[i].astype(BF16), row(g_post_mix[i]), row(g_pre_ffn[i]),
                  w_ffn_up[i].astype(BF16), w_ffn_down[i].astype(BF16), row(g_post_ffn[i]),
                  w_ple[i].astype(BF16), w_ple_gate[i].astype(BF16), row(g_post_ple[i])]
        x = _post_call(x, ya, yb, yc, gates, p[i], consts)
    return x
```

```python
import functools

import jax
import jax.numpy as jnp
import numpy as np
from jax import lax
from jax.experimental import pallas as pl
from jax.experimental.pallas import tpu as pltpu

F32 = jnp.float32
BF16 = jnp.bfloat16
I32 = jnp.int32
I16 = jnp.int16

N_HEADS = 8
HEAD_DIM = 64
ATTN_WIDTH = N_HEADS * HEAD_DIM
ROPE_DIM = HEAD_DIM // 4
ROPE_THETA = 500000.0
IDX_HEADS = 8
IDX_DIM = 64
INDEX_TOPK = 256
LRU_WIDTH = 512
LRU_BLOCKS = 8
CONV_WIDTH = 4
LRU_C = 8.0
GMLP_WIDTH = 512
GMLP_GROUPS = 8
GMLP_GROUP_DIM = GMLP_WIDTH // GMLP_GROUPS
CHUNK = 128
N_BRANCH = 3
EPS = 1e-6

LANES = 128
SUBLANES = 8
VMEM_LIMIT_BYTES = 56 * 1024 * 1024

INT_MIN = np.int32(-(2**31))
INT_MAX = np.int32(2**31 - 1)
MASK_BIAS = -1e30
Q_SCALE = float(np.log2(np.e)) * HEAD_DIM ** -0.5

PROJ_TOKENS = 512
DSA_ROWS = 256
DSA_KEYS = 512
DSA_SUB = 128
LRU_TOKENS = 512
POST_TOKENS = 256
FFN_CHUNK = 1024

SEG_Q, SEG_K, SEG_V, SEG_QI = 0, 512, 1024, 1536
SEG_KIWI = 2048
SEG_XR, SEG_GR, SEG_ZU, SEG_ZV = 2176, 2688, 3200, 3712
SEG_GATE = 4224


def _rms(x, g):
    return x * lax.rsqrt(jnp.mean(x * x, axis=-1, keepdims=True) + EPS) * g


def _dot(a, b):
    return jnp.dot(a, b, preferred_element_type=F32)


def _params(n_grid):
    return pltpu.CompilerParams(
        dimension_semantics=("arbitrary",) * n_grid,
        vmem_limit_bytes=VMEM_LIMIT_BYTES)


def _const_spec(shape):
    zeros = (0,) * len(shape)
    return pl.BlockSpec(shape, lambda *_: zeros, pipeline_mode=pl.Buffered(1))


def _rope_table_kernel(pos_ref, invf_ref, c_ref, s_ref):
    ang = pos_ref[...].astype(F32) * invf_ref[...]
    lane = lax.broadcasted_iota(I32, ang.shape, 1) & (HEAD_DIM - 1)
    cos = jnp.cos(ang)
    sin = jnp.sin(ang)
    half = ROPE_DIM // 2
    c_ref[...] = jnp.where(lane < ROPE_DIM, cos, 1.0)
    s_ref[...] = jnp.where(lane < half, -sin, jnp.where(lane < ROPE_DIM, sin, 0.0))


def _rope_tables(positions):
    n = positions.size
    tile = 1024
    half = ROPE_DIM // 2
    inv_freq = ROPE_THETA ** (-jnp.arange(half, dtype=F32) * 2.0 / ROPE_DIM)
    inv_lane = jnp.tile(inv_freq, LANES // half).reshape(1, LANES)
    out = jax.ShapeDtypeStruct((n, LANES), F32)
    return pl.pallas_call(
        _rope_table_kernel,
        grid=(n // tile,),
        in_specs=[pl.BlockSpec((tile, 1), lambda i: (i, 0)),
                  pl.BlockSpec((1, LANES), lambda i: (0, 0))],
        out_specs=[pl.BlockSpec((tile, LANES), lambda i: (i, 0))] * 2,
        out_shape=[out, out],
        compiler_params=_params(1),
        name="rope_tables",
    )(positions.reshape(n, 1), inv_lane)


def _proj_kernel(x_ref, g_ref, w_ref, c_ref, s_ref, gv_ref, wsp_ref, bsp_ref,
                 qt_ref, k_ref, vt_ref, qit_ref, ki_ref, wit_ref, xr_ref, gr_ref,
                 yc_ref, gate_ref):
    t = x_ref.shape[0]
    h = _rms(x_ref[...], g_ref[...]).astype(BF16)
    cos = c_ref[...]
    sin = s_ref[...]
    lane = lax.broadcasted_iota(I32, (t, LANES), 1)
    first_half = (lane & (HEAD_DIM - 1)) < (ROPE_DIM // 2)

    def proj(start, width):
        return _dot(h, w_ref[:, start:start + width])

    def rope(p):
        cols = []
        for j in range(p.shape[1] // LANES):
            pj = p[:, j * LANES:(j + 1) * LANES]
            partner = jnp.where(first_half,
                                pltpu.roll(pj, LANES - ROPE_DIM // 2, 1),
                                pltpu.roll(pj, ROPE_DIM // 2, 1))
            cols.append(pj * cos + partner * sin)
        return cols[0] if len(cols) == 1 else jnp.concatenate(cols, axis=1)

    qt_ref[...] = (rope(proj(SEG_Q, ATTN_WIDTH)) * Q_SCALE).T.astype(BF16)
    k_ref[...] = rope(proj(SEG_K, ATTN_WIDTH)).astype(BF16)
    vt_ref[...] = proj(SEG_V, ATTN_WIDTH).T.astype(BF16)
    qit_ref[...] = rope(proj(SEG_QI, IDX_HEADS * IDX_DIM)).T.astype(BF16)

    kiwi = proj(SEG_KIWI, LANES)
    kiwi = jnp.where(lane < IDX_DIM, rope(kiwi), kiwi)
    ki_ref[...] = jnp.where(lane < IDX_DIM, kiwi, pltpu.roll(kiwi, IDX_DIM, 1)).astype(BF16)
    wit_ref[...] = kiwi.T[IDX_DIM:IDX_DIM + IDX_HEADS, :]

    xr_ref[...] = proj(SEG_XR, LRU_WIDTH)
    gr_ref[...] = proj(SEG_GR, LRU_WIDTH)

    u = jax.nn.gelu(proj(SEG_ZU, GMLP_WIDTH))
    vn = _rms(jax.nn.gelu(proj(SEG_ZV, GMLP_WIDTH)), gv_ref[...]).astype(BF16)
    group = lax.broadcasted_iota(I32, (CHUNK, GMLP_WIDTH), 1) // GMLP_GROUP_DIM
    for c in range(t // CHUNK):
        vc = vn[c * CHUNK:(c + 1) * CHUNK, :]
        stack = jnp.concatenate(
            [jnp.where(group == g, vc, jnp.zeros_like(vc)) for g in range(GMLP_GROUPS)], axis=0)
        mixed = _dot(wsp_ref[...], stack) + bsp_ref[...]
        yc_ref[c * CHUNK:(c + 1) * CHUNK, :] = (u[c * CHUNK:(c + 1) * CHUNK, :] * mixed).astype(BF16)

    d = x_ref.shape[1]
    for n in range(N_BRANCH):
        gate_ref[:, n * d:(n + 1) * d] = jax.nn.sigmoid(proj(SEG_GATE + n * d, d)).astype(BF16)


def _proj_call(x, g_pre, w1, rope_c, rope_s, g_v, w_sp, b_sp):
    b, l, d = x.shape
    t = PROJ_TOKENS
    nt = l // t
    tok = lambda w: pl.BlockSpec((None, t, w), lambda bi, ti: (bi, ti, 0))
    tr = lambda w: pl.BlockSpec((None, w, t), lambda bi, ti: (bi, 0, ti))
    rope_spec = pl.BlockSpec((t, LANES), lambda bi, ti: (bi * nt + ti, 0))
    sds = lambda shape, dt: jax.ShapeDtypeStruct(shape, dt)
    return pl.pallas_call(
        _proj_kernel,
        grid=(b, nt),
        in_specs=[tok(d), _const_spec(g_pre.shape), _const_spec(w1.shape), rope_spec, rope_spec,
                  _const_spec(g_v.shape), _const_spec(w_sp.shape), _const_spec(b_sp.shape)],
        out_specs=[tr(ATTN_WIDTH), tok(ATTN_WIDTH), tr(ATTN_WIDTH), tr(IDX_HEADS * IDX_DIM),
                   tok(LANES), tr(IDX_HEADS), tok(LRU_WIDTH), tok(LRU_WIDTH), tok(GMLP_WIDTH),
                   tok(N_BRANCH * d)],
        out_shape=[sds((b, ATTN_WIDTH, l), BF16), sds((b, l, ATTN_WIDTH), BF16),
                   sds((b, ATTN_WIDTH, l), BF16), sds((b, IDX_HEADS * IDX_DIM, l), BF16),
                   sds((b, l, LANES), BF16), sds((b, IDX_HEADS, l), F32),
                   sds((b, l, LRU_WIDTH), F32), sds((b, l, LRU_WIDTH), F32),
                   sds((b, l, GMLP_WIDTH), BF16), sds((b, l, N_BRANCH * d), BF16)],
        compiler_params=_params(2),
        name="in_proj",
    )(x, g_pre, w1, rope_c, rope_s, g_v, w_sp, b_sp)


def _dsa_kernel(qit_ref, wit_ref, qt_ref, ki_ref, k_ref, vt_ref, o_ref,
                key_ref, hi_ref, lo_ref, s_ref, ot_ref, *, top_k):
    r = qt_ref.shape[1]
    l_keys = k_ref.shape[0]
    ck = DSA_KEYS
    sub = DSA_SUB
    nsub = ck // sub
    col0 = pl.program_id(1) * r
    nk = (col0 + r + ck - 1) // ck
    idx_scale = (IDX_DIM ** -0.5) * (IDX_HEADS ** -0.5)

    slab_row = lax.broadcasted_iota(I32, (LANES, r), 0)

    def head_weights(ref, h):
        slab = ref[(h // 2) * LANES:(h // 2 + 1) * LANES, :]
        keep = (slab_row < HEAD_DIM) if h % 2 == 0 else (slab_row >= HEAD_DIM)
        return jnp.where(keep, slab, jnp.zeros_like(slab))

    def chunk_loop(body, init):
        def wrapped(kc, carry):
            return body(pl.multiple_of(kc * ck, ck), carry)
        return lax.fori_loop(0, nk, wrapped, init)

    wit = wit_ref[...]
    w_idx = [head_weights(qit_ref, h) for h in range(IDX_HEADS)]
    qpos = col0 + lax.broadcasted_iota(I32, (sub, r), 1)
    krow = lax.broadcasted_iota(I32, (sub, r), 0)

    def score_chunk(k0, carry):
        for sb in range(nsub):
            k1 = k0 + sb * sub
            ki = ki_ref[pl.ds(k1, sub), :]
            acc = jnp.zeros((sub, r), F32)
            for h in range(IDX_HEADS):
                acc = acc + wit[h:h + 1, :] * jnp.maximum(_dot(ki, w_idx[h]), 0.0)
            bits = pltpu.bitcast(acc * idx_scale, I32)
            key = bits ^ ((bits >> 31) & INT_MAX)
            key = jnp.where(k1 + krow <= qpos, key, INT_MIN)
            key_ref[pl.ds(k1, sub), :] = key
            hi_ref[pl.ds(k1, sub), :] = (key >> 16).astype(I16)
            lo_ref[pl.ds(k1, sub), :] = ((key & 0xFFFF) - 32768).astype(I16)
        return carry

    chunk_loop(score_chunk, 0)

    n_acc = 4

    def count(ref, rows, pred):
        dt = ref.dtype
        one, zero = jnp.ones((), dt), jnp.zeros((), dt)

        def body(k0, accs):
            accs = list(accs)
            chunk = ref[pl.ds(k0, ck), :]
            for j in range(ck // rows):
                blk = chunk[j * rows:(j + 1) * rows, :]
                accs[j % n_acc] = accs[j % n_acc] + jnp.where(pred(blk, k0 + j * rows), one, zero)
            return tuple(accs)

        accs = chunk_loop(body, tuple(jnp.zeros((rows, r), dt) for _ in range(n_acc)))
        tot = accs[0].astype(I32)
        for a in accs[1:]:
            tot = tot + a.astype(I32)
        return jnp.sum(tot, axis=0, keepdims=True)

    def bisect16(ref, need):
        rows = 2 * SUBLANES

        def step(b, lo):
            cand = lo + lax.shift_left(jnp.int32(1), 15 - b)
            cand16 = jnp.broadcast_to(cand.astype(I16), (rows, r))
            n_ge = count(ref, rows, lambda blk, _: blk >= cand16)
            return jnp.where(n_ge >= need, cand, lo)

        return lax.fori_loop(0, 16, step, jnp.full((1, r), -32768, I32))

    thr_hi = bisect16(hi_ref, top_k)
    thr_hi16 = jnp.broadcast_to(thr_hi.astype(I16), (2 * SUBLANES, r))
    n_hi_gt = count(hi_ref, 2 * SUBLANES, lambda blk, _: blk > thr_hi16)

    def mask_low(k0, carry):
        sl = pl.ds(k0, ck)
        match = hi_ref[sl, :] == jnp.broadcast_to(thr_hi.astype(I16), (ck, r))
        lo_ref[sl, :] = jnp.where(match, lo_ref[sl, :], jnp.int16(-32768))
        return carry

    chunk_loop(mask_low, 0)
    thr_lo = bisect16(lo_ref, top_k - n_hi_gt)
    thr = thr_hi * 65536 + (thr_lo + 32768)

    thr8 = jnp.broadcast_to(thr, (SUBLANES, r))
    n_gt = count(key_ref, SUBLANES, lambda blk, _: blk > thr8)
    n_eq = count(key_ref, SUBLANES, lambda blk, _: blk == thr8)
    need = top_k - n_gt
    real = thr > INT_MIN
    has_tie = jnp.logical_and(real, n_eq > need)
    row8 = lax.broadcasted_iota(I32, (SUBLANES, r), 0)

    def tie_search():
        n_bits = max(1, int(l_keys - 1).bit_length())

        def pos_step(b, last):
            cand = last + lax.shift_left(jnp.int32(1), n_bits - 1 - b)
            n_before = count(key_ref, SUBLANES,
                             lambda blk, p0: jnp.logical_and(blk == thr8, p0 + row8 < cand))
            return jnp.where(n_before < need, cand, last)

        return lax.fori_loop(0, n_bits, pos_step, jnp.zeros((1, r), I32))

    any_tie = jnp.max(jnp.where(has_tie, 1.0, 0.0)) > 0.0
    key_row = lax.broadcasted_iota(I32, (ck, r), 0)

    def write_bias(bias_of):
        def bias_chunk(k0, carry):
            sl = pl.ds(k0, ck)
            key_ref[sl, :] = pltpu.bitcast(bias_of(key_ref[sl, :], k0), I32)
            return carry
        chunk_loop(bias_chunk, 0)

    def keep(selected):
        return jnp.where(selected, 0.0, MASK_BIAS).astype(F32)

    @pl.when(jnp.logical_not(any_tie))
    def _():
        thr_b = jnp.broadcast_to(jnp.where(real, thr, INT_MIN + 1), (ck, r))
        write_bias(lambda key, k0: keep(key >= thr_b))

    @pl.when(any_tie)
    def _():
        last = jnp.where(has_tie, tie_search(), jnp.where(real, INT_MAX, -1))
        thr_b = jnp.broadcast_to(thr, (ck, r))
        last_b = jnp.broadcast_to(last, (ck, r))
        write_bias(lambda key, k0: jnp.where(key == thr_b, keep(k0 + key_row <= last_b),
                                             keep(key > thr_b)))

    w_q = [head_weights(qt_ref, h) for h in range(N_HEADS)]
    pack = 2 * SUBLANES

    def fold8(a):
        out = a[0:SUBLANES, :]
        for j in range(1, a.shape[0] // SUBLANES):
            out = jnp.maximum(out, a[j * SUBLANES:(j + 1) * SUBLANES, :])
        return out

    def logits_chunk(k0, ms):
        ms = list(ms)
        for sb in range(nsub):
            k1 = k0 + sb * sub
            bias = pltpu.bitcast(key_ref[pl.ds(k1, sub), :], F32)
            for h in range(N_HEADS):
                pair = (h // 2) * LANES
                s = _dot(k_ref[pl.ds(k1, sub), pair:pair + LANES], w_q[h]) + bias
                s_ref[h, pl.ds(k1, sub), :] = s.astype(BF16)
                ms[h] = jnp.maximum(ms[h], fold8(s))
        return tuple(ms)

    ms = chunk_loop(logits_chunk,
                    tuple(jnp.full((SUBLANES, r), -jnp.inf, F32) for _ in range(N_HEADS)))
    ms = [jnp.broadcast_to(jnp.max(m, axis=0, keepdims=True).astype(BF16), (ck, r)) for m in ms]

    ot_ref[...] = jnp.zeros_like(ot_ref)
    ones_rows = jnp.ones((pack, ck), BF16)

    def pv_chunk(k0, carry):
        sl = pl.ds(k0, ck)
        for h in range(N_HEADS):
            p = jnp.exp2(s_ref[h, sl, :] - ms[h])
            lhs = jnp.concatenate([vt_ref[h * HEAD_DIM:(h + 1) * HEAD_DIM, sl], ones_rows], axis=0)
            ot_ref[h] += _dot(lhs, p)
        return carry

    chunk_loop(pv_chunk, 0)
    outs = []
    for h in range(N_HEADS):
        acc = ot_ref[h]
        outs.append(acc[0:HEAD_DIM, :] * (1.0 / acc[HEAD_DIM:HEAD_DIM + 1, :]))
    o_ref[...] = jnp.concatenate(outs, axis=0).T.astype(BF16)


def _dsa_call(qit, wit, qt, ki, k, vt):
    b, l, _ = k.shape
    r = DSA_ROWS
    top_k = min(INDEX_TOPK, l // 4)
    col = lambda h: pl.BlockSpec((None, h, r), lambda bi, ti: (bi, 0, ti))
    full = lambda h, w: pl.BlockSpec((None, h, w), lambda bi, ti: (bi, 0, 0),
                                     pipeline_mode=pl.Buffered(1))
    return pl.pallas_call(
        functools.partial(_dsa_kernel, top_k=top_k),
        grid=(b, l // r),
        in_specs=[col(IDX_HEADS * IDX_DIM), col(IDX_HEADS), col(ATTN_WIDTH),
                  full(l, LANES), full(l, ATTN_WIDTH), full(ATTN_WIDTH, l)],
        out_specs=pl.BlockSpec((None, r, ATTN_WIDTH), lambda bi, ti: (bi, ti, 0)),
        out_shape=jax.ShapeDtypeStruct((b, l, ATTN_WIDTH), BF16),
        scratch_shapes=[pltpu.VMEM((l, r), I32), pltpu.VMEM((l, r), I16), pltpu.VMEM((l, r), I16),
                        pltpu.VMEM((N_HEADS, l, r), BF16),
                        pltpu.VMEM((N_HEADS, HEAD_DIM + 2 * SUBLANES, r), F32)],
        compiler_params=_params(2),
        name="dsa_attention",
    )(qit, wit, qt, ki, k, vt)


def _lru_kernel(xr_ref, gr_ref, cw_ref, cb_ref, wa_ref, ba_ref, wx_ref, bx_ref, lam_ref,
                y_ref, hist_ref, h_ref):
    t = xr_ref.shape[0]

    @pl.when(pl.program_id(1) == 0)
    def _():
        hist_ref[...] = jnp.zeros_like(hist_ref)
        h_ref[...] = jnp.zeros_like(h_ref)

    x = xr_ref[...]
    hist = hist_ref[...]
    row = lax.broadcasted_iota(I32, x.shape, 0)
    row8 = lax.broadcasted_iota(I32, hist.shape, 0)
    cw = cw_ref[...]
    conv = x * cw[CONV_WIDTH - 1:CONV_WIDTH, :] + cb_ref[...]
    for d in range(1, CONV_WIDTH):
        shifted = pltpu.roll(x, d, 0)
        head = jnp.where(row8 < d, pltpu.roll(hist, d, 0), shifted[0:SUBLANES, :])
        shifted = jnp.concatenate([head, shifted[SUBLANES:, :]], axis=0)
        conv = conv + shifted * cw[CONV_WIDTH - 1 - d:CONV_WIDTH - d, :]
    hist_ref[...] = x[t - SUBLANES:, :]

    cb16 = conv.astype(BF16)
    rg = jax.nn.sigmoid(_dot(cb16, wa_ref[...]) + ba_ref[...])
    ig = jax.nn.sigmoid(_dot(cb16, wx_ref[...]) + bx_ref[...])
    log_a = -LRU_C * rg * jax.nn.softplus(-lam_ref[...])
    a = jnp.exp(log_a)
    inp = jnp.sqrt(-jnp.tanh(log_a) * (a * a + 1.0)) * (ig * conv)

    d = 1
    while d < t:
        ok = row >= d
        inp = jnp.where(ok, a * pltpu.roll(inp, d, 0) + inp, inp)
        a = jnp.where(ok, a * pltpu.roll(a, d, 0), a)
        d *= 2
    hcur = inp + a * h_ref[0:1, :]
    h_ref[...] = jnp.broadcast_to(hcur[t - 1:t, :], h_ref.shape)
    y_ref[...] = (hcur * jax.nn.gelu(gr_ref[...])).astype(BF16)


def _lru_call(xr, gr, conv_w, conv_b, wa, ba, wx, bx, lam):
    b, l, w = xr.shape
    t = LRU_TOKENS
    tok = pl.BlockSpec((None, t, w), lambda bi, ti: (bi, ti, 0))
    consts = [conv_w, conv_b, wa, ba, wx, bx, lam]
    return pl.pallas_call(
        _lru_kernel,
        grid=(b, l // t),
        in_specs=[tok, tok] + [_const_spec(c.shape) for c in consts],
        out_specs=tok,
        out_shape=jax.ShapeDtypeStruct((b, l, w), BF16),
        scratch_shapes=[pltpu.VMEM((SUBLANES, w), F32), pltpu.VMEM((SUBLANES, w), F32)],
        compiler_params=_params(2),
        name="conv_rglru",
    )(xr, gr, *consts)


def _post_kernel(x_ref, ya_ref, yb_ref, yc_ref, gate_ref, p_ref,
                 wb_ref, wo_ref, gpm_ref, gpf_ref, wup_ref, wdn_ref, gpo_ref,
                 wple_ref, wpg_ref, gpp_ref, o_ref):
    d = x_ref.shape[1]
    w = ya_ref.shape[1]
    x = x_ref[...]
    merged = jnp.zeros(x.shape, F32)
    for n, y_ref in enumerate((ya_ref, yb_ref, yc_ref)):
        branch = _dot(y_ref[...], wb_ref[n * w:(n + 1) * w, :])
        merged = merged + gate_ref[:, n * d:(n + 1) * d].astype(F32) * branch
    x = x + _rms(_dot(merged.astype(BF16), wo_ref[...]), gpm_ref[...])

    h2 = _rms(x, gpf_ref[...]).astype(BF16)
    f = jnp.zeros(x.shape, F32)
    for c in range(wup_ref.shape[1] // FFN_CHUNK):
        sl = slice(c * FFN_CHUNK, (c + 1) * FFN_CHUNK)
        hid = jnp.square(jnp.maximum(_dot(h2, wup_ref[:, sl]), 0.0)).astype(BF16)
        f = f + _dot(hid, wdn_ref[sl, :])
    x = x + _rms(f, gpo_ref[...])

    ple = _dot(p_ref[...].astype(BF16), wple_ref[...]) * jax.nn.sigmoid(
        _dot(x.astype(BF16), wpg_ref[...]))
    o_ref[...] = x + _rms(ple, gpp_ref[...])


def _post_call(x, ya, yb, yc, gates, p, consts):
    b, l, d = x.shape
    t = POST_TOKENS
    tok = lambda w: pl.BlockSpec((None, t, w), lambda bi, ti: (bi, ti, 0))
    return pl.pallas_call(
        _post_kernel,
        grid=(b, l // t),
        in_specs=[tok(d), tok(ya.shape[2]), tok(yb.shape[2]), tok(yc.shape[2]),
                  tok(gates.shape[2]), tok(p.shape[2])] + [_const_spec(c.shape) for c in consts],
        out_specs=tok(d),
        out_shape=jax.ShapeDtypeStruct((b, l, d), F32),
        compiler_params=_params(2),
        name="merge_ffn_ple",
    )(x, ya, yb, yc, gates, p, *consts)


def _block_diag(w):
    nb, bi, bo = w.shape
    eye = jnp.eye(nb, dtype=w.dtype)
    return jnp.einsum("hij,hg->higj", w, eye).reshape(nb * bi, nb * bo)


def _pack_w_in(w_in):
    d = w_in.shape[0]
    kiwi = w_in[:, SEG_KIWI:SEG_KIWI + IDX_DIM + IDX_HEADS]
    pad = jnp.zeros((d, LANES - IDX_DIM - IDX_HEADS), w_in.dtype)
    rest = w_in[:, SEG_KIWI + IDX_DIM + IDX_HEADS:]
    return jnp.concatenate([w_in[:, :SEG_KIWI], kiwi, pad, rest], axis=1).astype(BF16)


def kernel(x, p, positions, g_pre_mix, w_in, conv_w, conv_b, w_rg_a, b_rg_a, w_rg_x, b_rg_x,
           lru_lambda, g_gmlp_v, w_spatial, b_spatial, w_branch, w_out, g_post_mix,
           g_pre_ffn, w_ffn_up, w_ffn_down, g_post_ffn, w_ple, w_ple_gate, g_post_ple):
    depth = w_in.shape[0]
    b, l, d = x.shape
    assert l % PROJ_TOKENS == 0 and l % DSA_KEYS == 0 and l % LRU_TOKENS == 0 and l % DSA_ROWS == 0
    assert l % POST_TOKENS == 0 and (b * l) % 1024 == 0 and DSA_KEYS >= INDEX_TOPK
    row = lambda a: a.reshape(1, -1)

    rope_c, rope_s = _rope_tables(positions)
    tril = jnp.tril(jnp.ones((CHUNK, CHUNK), dtype=bool))

    for i in range(depth):
        w_sp = jnp.where(tril[None], w_spatial[i], 0)
        w_sp = jnp.transpose(w_sp, (1, 0, 2)).reshape(CHUNK, GMLP_GROUPS * CHUNK).astype(BF16)
        b_sp = jnp.repeat(jnp.transpose(b_spatial[i]), GMLP_GROUP_DIM, axis=1)

        qt, k, vt, qit, ki, wit, xr, gr, yc, gates = _proj_call(
            x, row(g_pre_mix[i]), _pack_w_in(w_in[i]), rope_c, rope_s,
            row(g_gmlp_v[i]), w_sp, b_sp)

        ya = _dsa_call(qit, wit, qt, ki, k, vt)

        yb = _lru_call(xr, gr, conv_w[i], row(conv_b[i]),
                       _block_diag(w_rg_a[i]).astype(BF16), row(b_rg_a[i]),
                       _block_diag(w_rg_x[i]).astype(BF16), row(b_rg_x[i]),
                       row(lru_lambda[i]))

        consts = [w_branch[i].reshape(N_BRANCH * w_branch.shape[2], d).astype(BF16),
                  w_out[i].astype(BF16), row(g_post_mix[i]), row(g_pre_ffn[i]),
                  w_ffn_up[i].astype(BF16), w_ffn_down[i].astype(BF16), row(g_post_ffn[i]),
                  w_ple[i].astype(BF16), w_ple_gate[i].astype(BF16), row(g_post_ple[i])]
        x = _post_call(x, ya, yb, yc, gates, p[i], consts)
    return x
```

```python
import functools

import jax
import jax.numpy as jnp
import numpy as np
from jax import lax
from jax.experimental import pallas as pl
from jax.experimental.pallas import tpu as pltpu

F32 = jnp.float32
BF16 = jnp.bfloat16
I32 = jnp.int32
I16 = jnp.int16

N_HEADS = 8
HEAD_DIM = 64
ATTN_WIDTH = N_HEADS * HEAD_DIM
ROPE_DIM = HEAD_DIM // 4
ROPE_THETA = 500000.0
IDX_HEADS = 8
IDX_DIM = 64
INDEX_TOPK = 256
LRU_WIDTH = 512
LRU_BLOCKS = 8
CONV_WIDTH = 4
LRU_C = 8.0
GMLP_WIDTH = 512
GMLP_GROUPS = 8
GMLP_GROUP_DIM = GMLP_WIDTH // GMLP_GROUPS
CHUNK = 128
N_BRANCH = 3
EPS = 1e-6

LANES = 128
SUBLANES = 8
VMEM_LIMIT_BYTES = 56 * 1024 * 1024

INT_MIN = np.int32(-(2**31))
INT_MAX = np.int32(2**31 - 1)
MASK_BIAS = -1e30
Q_SCALE = float(np.log2(np.e)) * HEAD_DIM ** -0.5

PROJ_TOKENS = 512
DSA_ROWS = 256
DSA_KEYS = 512
DSA_SUB = 128
LRU_TOKENS = 512
POST_TOKENS = 512
FFN_CHUNK = 1024

SEG_Q, SEG_K, SEG_V, SEG_QI = 0, 512, 1024, 1536
SEG_KIWI = 2048
SEG_XR, SEG_GR, SEG_ZU, SEG_ZV = 2176, 2688, 3200, 3712
SEG_GATE = 4224


def _rms(x, g):
    return x * lax.rsqrt(jnp.mean(x * x, axis=-1, keepdims=True) + EPS) * g


def _dot(a, b):
    return jnp.dot(a, b, preferred_element_type=F32)


def _params(n_grid):
    return pltpu.CompilerParams(
        dimension_semantics=("arbitrary",) * n_grid,
        vmem_limit_bytes=VMEM_LIMIT_BYTES)


def _const_spec(shape):
    zeros = (0,) * len(shape)
    return pl.BlockSpec(shape, lambda *_: zeros, pipeline_mode=pl.Buffered(1))


def _rope_table_kernel(pos_ref, invf_ref, c_ref, s_ref):
    ang = pos_ref[...].astype(F32) * invf_ref[...]
    lane = lax.broadcasted_iota(I32, ang.shape, 1) & (HEAD_DIM - 1)
    cos = jnp.cos(ang)
    sin = jnp.sin(ang)
    half = ROPE_DIM // 2
    c_ref[...] = jnp.where(lane < ROPE_DIM, cos, 1.0)
    s_ref[...] = jnp.where(lane < half, -sin, jnp.where(lane < ROPE_DIM, sin, 0.0))


def _rope_tables(positions):
    n = positions.size
    tile = 1024
    half = ROPE_DIM // 2
    inv_freq = ROPE_THETA ** (-jnp.arange(half, dtype=F32) * 2.0 / ROPE_DIM)
    inv_lane = jnp.tile(inv_freq, LANES // half).reshape(1, LANES)
    out = jax.ShapeDtypeStruct((n, LANES), F32)
    return pl.pallas_call(
        _rope_table_kernel,
        grid=(n // tile,),
        in_specs=[pl.BlockSpec((tile, 1), lambda i: (i, 0)),
                  pl.BlockSpec((1, LANES), lambda i: (0, 0))],
        out_specs=[pl.BlockSpec((tile, LANES), lambda i: (i, 0))] * 2,
        out_shape=[out, out],
        compiler_params=_params(1),
        name="rope_tables",
    )(positions.reshape(n, 1), inv_lane)


def _proj_kernel(x_ref, g_ref, w_ref, c_ref, s_ref, gv_ref, wsp_ref, bsp_ref,
                 qt_ref, k_ref, vt_ref, qit_ref, ki_ref, wit_ref, xr_ref, gr_ref,
                 yc_ref, gate_ref):
    t = x_ref.shape[0]
    h = _rms(x_ref[...], g_ref[...]).astype(BF16)
    cos = c_ref[...]
    sin = s_ref[...]
    lane = lax.broadcasted_iota(I32, (t, LANES), 1)
    first_half = (lane & (HEAD_DIM - 1)) < (ROPE_DIM // 2)

    def proj(start, width):
        return _dot(h, w_ref[:, start:start + width])

    def rope(p):
        cols = []
        for j in range(p.shape[1] // LANES):
            pj = p[:, j * LANES:(j + 1) * LANES]
            partner = jnp.where(first_half,
                                pltpu.roll(pj, LANES - ROPE_DIM // 2, 1),
                                pltpu.roll(pj, ROPE_DIM // 2, 1))
            cols.append(pj * cos + partner * sin)
        return cols[0] if len(cols) == 1 else jnp.concatenate(cols, axis=1)

    qt_ref[...] = (rope(proj(SEG_Q, ATTN_WIDTH)) * Q_SCALE).T.astype(BF16)
    k_ref[...] = rope(proj(SEG_K, ATTN_WIDTH)).astype(BF16)
    vt_ref[...] = proj(SEG_V, ATTN_WIDTH).T.astype(BF16)
    qit_ref[...] = rope(proj(SEG_QI, IDX_HEADS * IDX_DIM)).T.astype(BF16)

    kiwi = proj(SEG_KIWI, LANES)
    kiwi = jnp.where(lane < IDX_DIM, rope(kiwi), kiwi)
    ki_ref[...] = jnp.where(lane < IDX_DIM, kiwi, pltpu.roll(kiwi, IDX_DIM, 1)).astype(BF16)
    wit_ref[...] = kiwi.T[IDX_DIM:IDX_DIM + IDX_HEADS, :]

    xr_ref[...] = proj(SEG_XR, LRU_WIDTH)
    gr_ref[...] = proj(SEG_GR, LRU_WIDTH)

    u = jax.nn.gelu(proj(SEG_ZU, GMLP_WIDTH))
    vn = _rms(jax.nn.gelu(proj(SEG_ZV, GMLP_WIDTH)), gv_ref[...]).astype(BF16)
    group = lax.broadcasted_iota(I32, (CHUNK, GMLP_WIDTH), 1) // GMLP_GROUP_DIM
    for c in range(t // CHUNK):
        vc = vn[c * CHUNK:(c + 1) * CHUNK, :]
        stack = jnp.concatenate(
            [jnp.where(group == g, vc, jnp.zeros_like(vc)) for g in range(GMLP_GROUPS)], axis=0)
        mixed = _dot(wsp_ref[...], stack) + bsp_ref[...]
        yc_ref[c * CHUNK:(c + 1) * CHUNK, :] = (u[c * CHUNK:(c + 1) * CHUNK, :] * mixed).astype(BF16)

    d = x_ref.shape[1]
    for n in range(N_BRANCH):
        gate_ref[:, n * d:(n + 1) * d] = jax.nn.sigmoid(proj(SEG_GATE + n * d, d)).astype(BF16)


def _proj_call(x, g_pre, w1, rope_c, rope_s, g_v, w_sp, b_sp):
    b, l, d = x.shape
    t = PROJ_TOKENS
    nt = l // t
    tok = lambda w: pl.BlockSpec((None, t, w), lambda bi, ti: (bi, ti, 0))
    tr = lambda w: pl.BlockSpec((None, w, t), lambda bi, ti: (bi, 0, ti))
    rope_spec = pl.BlockSpec((t, LANES), lambda bi, ti: (bi * nt + ti, 0))
    sds = lambda shape, dt: jax.ShapeDtypeStruct(shape, dt)
    return pl.pallas_call(
        _proj_kernel,
        grid=(b, nt),
        in_specs=[tok(d), _const_spec(g_pre.shape), _const_spec(w1.shape), rope_spec, rope_spec,
                  _const_spec(g_v.shape), _const_spec(w_sp.shape), _const_spec(b_sp.shape)],
        out_specs=[tr(ATTN_WIDTH), tok(ATTN_WIDTH), tr(ATTN_WIDTH), tr(IDX_HEADS * IDX_DIM),
                   tok(LANES), tr(IDX_HEADS), tok(LRU_WIDTH), tok(LRU_WIDTH), tok(GMLP_WIDTH),
                   tok(N_BRANCH * d)],
        out_shape=[sds((b, ATTN_WIDTH, l), BF16), sds((b, l, ATTN_WIDTH), BF16),
                   sds((b, ATTN_WIDTH, l), BF16), sds((b, IDX_HEADS * IDX_DIM, l), BF16),
                   sds((b, l, LANES), BF16), sds((b, IDX_HEADS, l), F32),
                   sds((b, l, LRU_WIDTH), F32), sds((b, l, LRU_WIDTH), F32),
                   sds((b, l, GMLP_WIDTH), BF16), sds((b, l, N_BRANCH * d), BF16)],
        compiler_params=_params(2),
        name="in_proj",
    )(x, g_pre, w1, rope_c, rope_s, g_v, w_sp, b_sp)


def _dsa_kernel(qit_ref, wit_ref, qt_ref, ki_ref, k_ref, vt_ref, o_ref,
                key_ref, hi_ref, lo_ref, s_ref, ot_ref, *, top_k):
    r = qt_ref.shape[1]
    l_keys = k_ref.shape[0]
    ck = DSA_KEYS
    sub = DSA_SUB
    nsub = ck // sub
    col0 = pl.program_id(1) * r
    nk = (col0 + r + ck - 1) // ck
    idx_scale = (IDX_DIM ** -0.5) * (IDX_HEADS ** -0.5)

    slab_row = lax.broadcasted_iota(I32, (LANES, r), 0)

    def head_weights(ref, h):
        slab = ref[(h // 2) * LANES:(h // 2 + 1) * LANES, :]
        keep = (slab_row < HEAD_DIM) if h % 2 == 0 else (slab_row >= HEAD_DIM)
        return jnp.where(keep, slab, jnp.zeros_like(slab))

    def chunk_loop(body, init):
        def wrapped(kc, carry):
            return body(pl.multiple_of(kc * ck, ck), carry)
        return lax.fori_loop(0, nk, wrapped, init)

    wit = wit_ref[...]
    w_idx = [head_weights(qit_ref, h) for h in range(IDX_HEADS)]
    qpos = col0 + lax.broadcasted_iota(I32, (sub, r), 1)
    krow = lax.broadcasted_iota(I32, (sub, r), 0)

    def score_chunk(k0, carry):
        for sb in range(nsub):
            k1 = k0 + sb * sub
            ki = ki_ref[pl.ds(k1, sub), :]
            acc = jnp.zeros((sub, r), F32)
            for h in range(IDX_HEADS):
                acc = acc + wit[h:h + 1, :] * jnp.maximum(_dot(ki, w_idx[h]), 0.0)
            bits = pltpu.bitcast(acc * idx_scale, I32)
            key = bits ^ ((bits >> 31) & INT_MAX)
            key = jnp.where(k1 + krow <= qpos, key, INT_MIN)
            key_ref[pl.ds(k1, sub), :] = key
            hi_ref[pl.ds(k1, sub), :] = (key >> 16).astype(I16)
            lo_ref[pl.ds(k1, sub), :] = ((key & 0xFFFF) - 32768).astype(I16)
        return carry

    chunk_loop(score_chunk, 0)

    n_acc = 4

    def count(ref, rows, pred):
        dt = ref.dtype
        one, zero = jnp.ones((), dt), jnp.zeros((), dt)

        def body(k0, accs):
            accs = list(accs)
            chunk = ref[pl.ds(k0, ck), :]
            for j in range(ck // rows):
                blk = chunk[j * rows:(j + 1) * rows, :]
                accs[j % n_acc] = accs[j % n_acc] + jnp.where(pred(blk, k0 + j * rows), one, zero)
            return tuple(accs)

        accs = chunk_loop(body, tuple(jnp.zeros((rows, r), dt) for _ in range(n_acc)))
        tot = accs[0].astype(I32)
        for a in accs[1:]:
            tot = tot + a.astype(I32)
        return jnp.sum(tot, axis=0, keepdims=True)

    def bisect16(ref, need):
        rows = 2 * SUBLANES

        def step(b, lo):
            cand = lo + lax.shift_left(jnp.int32(1), 15 - b)
            cand16 = jnp.broadcast_to(cand.astype(I16), (rows, r))
            n_ge = count(ref, rows, lambda blk, _: blk >= cand16)
            return jnp.where(n_ge >= need, cand, lo)

        return lax.fori_loop(0, 16, step, jnp.full((1, r), -32768, I32))

    thr_hi = bisect16(hi_ref, top_k)
    thr_hi16 = jnp.broadcast_to(thr_hi.astype(I16), (2 * SUBLANES, r))
    n_hi_gt = count(hi_ref, 2 * SUBLANES, lambda blk, _: blk > thr_hi16)

    def mask_low(k0, carry):
        sl = pl.ds(k0, ck)
        match = hi_ref[sl, :] == jnp.broadcast_to(thr_hi.astype(I16), (ck, r))
        lo_ref[sl, :] = jnp.where(match, lo_ref[sl, :], jnp.int16(-32768))
        return carry

    chunk_loop(mask_low, 0)
    thr_lo = bisect16(lo_ref, top_k - n_hi_gt)
    thr = thr_hi * 65536 + (thr_lo + 32768)

    thr8 = jnp.broadcast_to(thr, (SUBLANES, r))
    n_gt = count(key_ref, SUBLANES, lambda blk, _: blk > thr8)
    n_eq = count(key_ref, SUBLANES, lambda blk, _: blk == thr8)
    need = top_k - n_gt
    real = thr > INT_MIN
    has_tie = jnp.logical_and(real, n_eq > need)
    row8 = lax.broadcasted_iota(I32, (SUBLANES, r), 0)

    def tie_search():
        n_bits = max(1, int(l_keys - 1).bit_length())

        def pos_step(b, last):
            cand = last + lax.shift_left(jnp.int32(1), n_bits - 1 - b)
            n_before = count(key_ref, SUBLANES,
                             lambda blk, p0: jnp.logical_and(blk == thr8, p0 + row8 < cand))
            return jnp.where(n_before < need, cand, last)

        return lax.fori_loop(0, n_bits, pos_step, jnp.zeros((1, r), I32))

    any_tie = jnp.max(jnp.where(has_tie, 1.0, 0.0)) > 0.0
    key_row = lax.broadcasted_iota(I32, (ck, r), 0)

    def write_bias(bias_of):
        def bias_chunk(k0, carry):
            sl = pl.ds(k0, ck)
            key_ref[sl, :] = pltpu.bitcast(bias_of(key_ref[sl, :], k0), I32)
            return carry
        chunk_loop(bias_chunk, 0)

    def keep(selected):
        return jnp.where(selected, 0.0, MASK_BIAS).astype(F32)

    @pl.when(jnp.logical_not(any_tie))
    def _():
        thr_b = jnp.broadcast_to(jnp.where(real, thr, INT_MIN + 1), (ck, r))
        write_bias(lambda key, k0: keep(key >= thr_b))

    @pl.when(any_tie)
    def _():
        last = jnp.where(has_tie, tie_search(), jnp.where(real, INT_MAX, -1))
        thr_b = jnp.broadcast_to(thr, (ck, r))
        last_b = jnp.broadcast_to(last, (ck, r))
        write_bias(lambda key, k0: jnp.where(key == thr_b, keep(k0 + key_row <= last_b),
                                             keep(key > thr_b)))

    w_q = [head_weights(qt_ref, h) for h in range(N_HEADS)]
    pack = 2 * SUBLANES

    def fold8(a):
        out = a[0:SUBLANES, :]
        for j in range(1, a.shape[0] // SUBLANES):
            out = jnp.maximum(out, a[j * SUBLANES:(j + 1) * SUBLANES, :])
        return out

    def logits_chunk(k0, ms):
        ms = list(ms)
        for sb in range(nsub):
            k1 = k0 + sb * sub
            bias = pltpu.bitcast(key_ref[pl.ds(k1, sub), :], F32)
            for h in range(N_HEADS):
                pair = (h // 2) * LANES
                s = _dot(k_ref[pl.ds(k1, sub), pair:pair + LANES], w_q[h]) + bias
                s_ref[h, pl.ds(k1, sub), :] = s.astype(BF16)
                ms[h] = jnp.maximum(ms[h], fold8(s))
        return tuple(ms)

    ms = chunk_loop(logits_chunk,
                    tuple(jnp.full((SUBLANES, r), -jnp.inf, F32) for _ in range(N_HEADS)))
    ms = [jnp.broadcast_to(jnp.max(m, axis=0, keepdims=True).astype(BF16), (ck, r)) for m in ms]

    ot_ref[...] = jnp.zeros_like(ot_ref)
    ones_rows = jnp.ones((pack, ck), BF16)

    def pv_chunk(k0, carry):
        sl = pl.ds(k0, ck)
        for h in range(N_HEADS):
            p = jnp.exp2(s_ref[h, sl, :] - ms[h])
            lhs = jnp.concatenate([vt_ref[h * HEAD_DIM:(h + 1) * HEAD_DIM, sl], ones_rows], axis=0)
            ot_ref[h] += _dot(lhs, p)
        return carry

    chunk_loop(pv_chunk, 0)
    outs = []
    for h in range(N_HEADS):
        acc = ot_ref[h]
        outs.append(acc[0:HEAD_DIM, :] * (1.0 / acc[HEAD_DIM:HEAD_DIM + 1, :]))
    o_ref[...] = jnp.concatenate(outs, axis=0).T.astype(BF16)


def _dsa_call(qit, wit, qt, ki, k, vt):
    b, l, _ = k.shape
    r = DSA_ROWS
    top_k = min(INDEX_TOPK, l // 4)
    col = lambda h: pl.BlockSpec((None, h, r), lambda bi, ti: (bi, 0, ti))
    full = lambda h, w: pl.BlockSpec((None, h, w), lambda bi, ti: (bi, 0, 0),
                                     pipeline_mode=pl.Buffered(1))
    return pl.pallas_call(
        functools.partial(_dsa_kernel, top_k=top_k),
        grid=(b, l // r),
        in_specs=[col(IDX_HEADS * IDX_DIM), col(IDX_HEADS), col(ATTN_WIDTH),
                  full(l, LANES), full(l, ATTN_WIDTH), full(ATTN_WIDTH, l)],
        out_specs=pl.BlockSpec((None, r, ATTN_WIDTH), lambda bi, ti: (bi, ti, 0)),
        out_shape=jax.ShapeDtypeStruct((b, l, ATTN_WIDTH), BF16),
        scratch_shapes=[pltpu.VMEM((l, r), I32), pltpu.VMEM((l, r), I16), pltpu.VMEM((l, r), I16),
                        pltpu.VMEM((N_HEADS, l, r), BF16),
                        pltpu.VMEM((N_HEADS, HEAD_DIM + 2 * SUBLANES, r), F32)],
        compiler_params=_params(2),
        name="dsa_attention",
    )(qit, wit, qt, ki, k, vt)


def _lru_kernel(xr_ref, gr_ref, cw_ref, cb_ref, wa_ref, ba_ref, wx_ref, bx_ref, lam_ref,
                y_ref, hist_ref, h_ref):
    t = xr_ref.shape[0]

    @pl.when(pl.program_id(1) == 0)
    def _():
        hist_ref[...] = jnp.zeros_like(hist_ref)
        h_ref[...] = jnp.zeros_like(h_ref)

    x = xr_ref[...]
    hist = hist_ref[...]
    row = lax.broadcasted_iota(I32, x.shape, 0)
    row8 = lax.broadcasted_iota(I32, hist.shape, 0)
    cw = cw_ref[...]
    conv = x * cw[CONV_WIDTH - 1:CONV_WIDTH, :] + cb_ref[...]
    for d in range(1, CONV_WIDTH):
        shifted = pltpu.roll(x, d, 0)
        head = jnp.where(row8 < d, pltpu.roll(hist, d, 0), shifted[0:SUBLANES, :])
        shifted = jnp.concatenate([head, shifted[SUBLANES:, :]], axis=0)
        conv = conv + shifted * cw[CONV_WIDTH - 1 - d:CONV_WIDTH - d, :]
    hist_ref[...] = x[t - SUBLANES:, :]

    cb16 = conv.astype(BF16)
    rg = jax.nn.sigmoid(_dot(cb16, wa_ref[...]) + ba_ref[...])
    ig = jax.nn.sigmoid(_dot(cb16, wx_ref[...]) + bx_ref[...])
    log_a = -LRU_C * rg * jax.nn.softplus(-lam_ref[...])
    a = jnp.exp(log_a)
    inp = jnp.sqrt(-jnp.tanh(log_a) * (a * a + 1.0)) * (ig * conv)

    in_group = row & (SUBLANES - 1)
    d = 1
    while d < SUBLANES:
        ok = in_group >= d
        inp = jnp.where(ok, a * pltpu.roll(inp, d, 0) + inp, inp)
        a = jnp.where(ok, a * pltpu.roll(a, d, 0), a)
        d *= 2
    carry = h_ref[0:1, :]
    gelu_g = jax.nn.gelu(gr_ref[...])
    pack = 2 * SUBLANES
    for g in range(t // pack):
        halves = []
        for half in range(2):
            rows = slice(g * pack + half * SUBLANES, g * pack + (half + 1) * SUBLANES)
            hg = inp[rows, :] + a[rows, :] * carry
            carry = hg[SUBLANES - 1:SUBLANES, :]
            halves.append(hg * gelu_g[rows, :])
        y_ref[g * pack:(g + 1) * pack, :] = jnp.concatenate(halves, axis=0).astype(BF16)
    h_ref[...] = jnp.broadcast_to(carry, h_ref.shape)


def _lru_call(xr, gr, conv_w, conv_b, wa, ba, wx, bx, lam):
    b, l, w = xr.shape
    t = LRU_TOKENS
    tok = pl.BlockSpec((None, t, w), lambda bi, ti: (bi, ti, 0))
    consts = [conv_w, conv_b, wa, ba, wx, bx, lam]
    return pl.pallas_call(
        _lru_kernel,
        grid=(b, l // t),
        in_specs=[tok, tok] + [_const_spec(c.shape) for c in consts],
        out_specs=tok,
        out_shape=jax.ShapeDtypeStruct((b, l, w), BF16),
        scratch_shapes=[pltpu.VMEM((SUBLANES, w), F32), pltpu.VMEM((SUBLANES, w), F32)],
        compiler_params=_params(2),
        name="conv_rglru",
    )(xr, gr, *consts)


def _post_kernel(x_ref, ya_ref, yb_ref, yc_ref, gate_ref, p_ref,
                 wb_ref, wo_ref, gpm_ref, gpf_ref, wup_ref, wdn_ref, gpo_ref,
                 wple_ref, wpg_ref, gpp_ref, o_ref):
    d = x_ref.shape[1]
    w = ya_ref.shape[1]
    x = x_ref[...]
    merged = jnp.zeros(x.shape, F32)
    for n, y_ref in enumerate((ya_ref, yb_ref, yc_ref)):
        branch = _dot(y_ref[...], wb_ref[n * w:(n + 1) * w, :])
        merged = merged + gate_ref[:, n * d:(n + 1) * d].astype(F32) * branch
    x = x + _rms(_dot(merged.astype(BF16), wo_ref[...]), gpm_ref[...])

    h2 = _rms(x, gpf_ref[...]).astype(BF16)
    f = jnp.zeros(x.shape, F32)
    for c in range(wup_ref.shape[1] // FFN_CHUNK):
        sl = slice(c * FFN_CHUNK, (c + 1) * FFN_CHUNK)
        hid = jnp.square(jnp.maximum(_dot(h2, wup_ref[:, sl]), 0.0)).astype(BF16)
        f = f + _dot(hid, wdn_ref[sl, :])
    x = x + _rms(f, gpo_ref[...])

    ple = _dot(p_ref[...].astype(BF16), wple_ref[...]) * jax.nn.sigmoid(
        _dot(x.astype(BF16), wpg_ref[...]))
    o_ref[...] = x + _rms(ple, gpp_ref[...])


def _post_call(x, ya, yb, yc, gates, p, consts):
    b, l, d = x.shape
    t = POST_TOKENS
    tok = lambda w: pl.BlockSpec((None, t, w), lambda bi, ti: (bi, ti, 0))
    return pl.pallas_call(
        _post_kernel,
        grid=(b, l // t),
        in_specs=[tok(d), tok(ya.shape[2]), tok(yb.shape[2]), tok(yc.shape[2]),
                  tok(gates.shape[2]), tok(p.shape[2])] + [_const_spec(c.shape) for c in consts],
        out_specs=tok(d),
        out_shape=jax.ShapeDtypeStruct((b, l, d), F32),
        compiler_params=_params(2),
        name="merge_ffn_ple",
    )(x, ya, yb, yc, gates, p, *consts)


def _block_diag(w):
    nb, bi, bo = w.shape
    eye = jnp.eye(nb, dtype=w.dtype)
    return jnp.einsum("hij,hg->higj", w, eye).reshape(nb * bi, nb * bo)


def _pack_w_in(w_in):
    w_in = w_in.astype(BF16)
    d = w_in.shape[0]
    kiwi = w_in[:, SEG_KIWI:SEG_KIWI + IDX_DIM + IDX_HEADS]
    pad = jnp.zeros((d, LANES - IDX_DIM - IDX_HEADS), w_in.dtype)
    rest = w_in[:, SEG_KIWI + IDX_DIM + IDX_HEADS:]
    return jnp.concatenate([w_in[:, :SEG_KIWI], kiwi, pad, rest], axis=1)


def kernel(x, p, positions, g_pre_mix, w_in, conv_w, conv_b, w_rg_a, b_rg_a, w_rg_x, b_rg_x,
           lru_lambda, g_gmlp_v, w_spatial, b_spatial, w_branch, w_out, g_post_mix,
           g_pre_ffn, w_ffn_up, w_ffn_down, g_post_ffn, w_ple, w_ple_gate, g_post_ple):
    depth = w_in.shape[0]
    b, l, d = x.shape
    assert l % PROJ_TOKENS == 0 and l % DSA_KEYS == 0 and l % LRU_TOKENS == 0 and l % DSA_ROWS == 0
    assert l % POST_TOKENS == 0 and (b * l) % 1024 == 0 and DSA_KEYS >= INDEX_TOPK
    row = lambda a: a.reshape(1, -1)

    rope_c, rope_s = _rope_tables(positions)
    tril = jnp.tril(jnp.ones((CHUNK, CHUNK), dtype=bool))

    for i in range(depth):
        w_sp = jnp.where(tril[None], w_spatial[i], 0)
        w_sp = jnp.transpose(w_sp, (1, 0, 2)).reshape(CHUNK, GMLP_GROUPS * CHUNK).astype(BF16)
        b_sp = jnp.repeat(jnp.transpose(b_spatial[i]), GMLP_GROUP_DIM, axis=1)

        qt, k, vt, qit, ki, wit, xr, gr, yc, gates = _proj_call(
            x, row(g_pre_mix[i]), _pack_w_in(w_in[i]), rope_c, rope_s,
            row(g_gmlp_v[i]), w_sp, b_sp)

        ya = _dsa_call(qit, wit, qt, ki, k, vt)

        yb = _lru_call(xr, gr, conv_w[i], row(conv_b[i]),
                       _block_diag(w_rg_a[i]).astype(BF16), row(b_rg_a[i]),
                       _block_diag(w_rg_x[i]).astype(BF16), row(b_rg_x[i]),
                       row(lru_lambda[i]))

        consts = [w_branch[i].reshape(N_BRANCH * w_branch.shape[2], d).astype(BF16),
                  w_out[i].astype(BF16), row(g_post_mix[i]), row(g_pre_ffn[i]),
                  w_ffn_up[i].astype(BF16), w_ffn_down[i].astype(BF16), row(g_post_ffn[i]),
                  w_ple[i].astype(BF16), w_ple_gate[i].astype(BF16), row(g_post_ple[i])]
        x = _post_call(x, ya, yb, yc, gates, p[i], consts)
    return x
```

```python
import functools

import jax
import jax.numpy as jnp
import numpy as np
from jax import lax
from jax.experimental import pallas as pl
from jax.experimental.pallas import tpu as pltpu

F32 = jnp.float32
BF16 = jnp.bfloat16
I32 = jnp.int32
I16 = jnp.int16

N_HEADS = 8
HEAD_DIM = 64
ATTN_WIDTH = N_HEADS * HEAD_DIM
ROPE_DIM = HEAD_DIM // 4
ROPE_THETA = 500000.0
IDX_HEADS = 8
IDX_DIM = 64
INDEX_TOPK = 256
LRU_WIDTH = 512
LRU_BLOCKS = 8
CONV_WIDTH = 4
LRU_C = 8.0
GMLP_WIDTH = 512
GMLP_GROUPS = 8
GMLP_GROUP_DIM = GMLP_WIDTH // GMLP_GROUPS
CHUNK = 128
N_BRANCH = 3
EPS = 1e-6

LANES = 128
SUBLANES = 8
VMEM_LIMIT_BYTES = 56 * 1024 * 1024

INT_MIN = np.int32(-(2**31))
INT_MAX = np.int32(2**31 - 1)
MASK_BIAS = -1e30
Q_SCALE = float(np.log2(np.e)) * HEAD_DIM ** -0.5

PROJ_TOKENS = 512
DSA_ROWS = 256
DSA_KEYS = 512
DSA_SUB = 128
LRU_TOKENS = 512
POST_TOKENS = 512
FFN_CHUNK = 1024

SEG_Q, SEG_K, SEG_V, SEG_QI = 0, 512, 1024, 1536
SEG_KIWI = 2048
SEG_XR, SEG_GR, SEG_ZU, SEG_ZV = 2176, 2688, 3200, 3712
SEG_GATE = 4224


def _rms(x, g):
    return x * lax.rsqrt(jnp.mean(x * x, axis=-1, keepdims=True) + EPS) * g


def _dot(a, b):
    return jnp.dot(a, b, preferred_element_type=F32)


def _params(n_grid):
    return pltpu.CompilerParams(
        dimension_semantics=("arbitrary",) * n_grid,
        vmem_limit_bytes=VMEM_LIMIT_BYTES)


def _const_spec(shape):
    zeros = (0,) * len(shape)
    return pl.BlockSpec(shape, lambda *_: zeros, pipeline_mode=pl.Buffered(1))


def _rope_table_kernel(pos_ref, invf_ref, c_ref, s_ref):
    ang = pos_ref[...].astype(F32) * invf_ref[...]
    lane = lax.broadcasted_iota(I32, ang.shape, 1) & (HEAD_DIM - 1)
    cos = jnp.cos(ang)
    sin = jnp.sin(ang)
    half = ROPE_DIM // 2
    c_ref[...] = jnp.where(lane < ROPE_DIM, cos, 1.0)
    s_ref[...] = jnp.where(lane < half, -sin, jnp.where(lane < ROPE_DIM, sin, 0.0))


def _rope_tables(positions):
    n = positions.size
    tile = 1024
    half = ROPE_DIM // 2
    inv_freq = ROPE_THETA ** (-jnp.arange(half, dtype=F32) * 2.0 / ROPE_DIM)
    inv_lane = jnp.tile(inv_freq, LANES // half).reshape(1, LANES)
    out = jax.ShapeDtypeStruct((n, LANES), F32)
    return pl.pallas_call(
        _rope_table_kernel,
        grid=(n // tile,),
        in_specs=[pl.BlockSpec((tile, 1), lambda i: (i, 0)),
                  pl.BlockSpec((1, LANES), lambda i: (0, 0))],
        out_specs=[pl.BlockSpec((tile, LANES), lambda i: (i, 0))] * 2,
        out_shape=[out, out],
        compiler_params=_params(1),
        name="rope_tables",
    )(positions.reshape(n, 1), inv_lane)


def _proj_kernel(x_ref, g_ref, w_ref, c_ref, s_ref, gv_ref, wsp_ref, bsp_ref,
                 qt_ref, k_ref, vt_ref, qit_ref, ki_ref, wit_ref, xr_ref, gr_ref,
                 yc_ref, gate_ref):
    t = x_ref.shape[0]
    h = _rms(x_ref[...], g_ref[...]).astype(BF16)
    cos = c_ref[...]
    sin = s_ref[...]
    lane = lax.broadcasted_iota(I32, (t, LANES), 1)
    first_half = (lane & (HEAD_DIM - 1)) < (ROPE_DIM // 2)

    def proj(start, width):
        return _dot(h, w_ref[:, start:start + width])

    def rope(p):
        cols = []
        for j in range(p.shape[1] // LANES):
            pj = p[:, j * LANES:(j + 1) * LANES]
            partner = jnp.where(first_half,
                                pltpu.roll(pj, LANES - ROPE_DIM // 2, 1),
                                pltpu.roll(pj, ROPE_DIM // 2, 1))
            cols.append(pj * cos + partner * sin)
        return cols[0] if len(cols) == 1 else jnp.concatenate(cols, axis=1)

    qt_ref[...] = (rope(proj(SEG_Q, ATTN_WIDTH)) * Q_SCALE).T.astype(BF16)
    k_ref[...] = rope(proj(SEG_K, ATTN_WIDTH)).astype(BF16)
    vt_ref[...] = proj(SEG_V, ATTN_WIDTH).T.astype(BF16)
    qit_ref[...] = rope(proj(SEG_QI, IDX_HEADS * IDX_DIM)).T.astype(BF16)

    kiwi = proj(SEG_KIWI, LANES)
    kiwi = jnp.where(lane < IDX_DIM, rope(kiwi), kiwi)
    ki_ref[...] = jnp.where(lane < IDX_DIM, kiwi, pltpu.roll(kiwi, IDX_DIM, 1)).astype(BF16)
    wit_ref[...] = kiwi.T[IDX_DIM:IDX_DIM + IDX_HEADS, :]

    xr_ref[...] = proj(SEG_XR, LRU_WIDTH)
    gr_ref[...] = proj(SEG_GR, LRU_WIDTH)

    u = jax.nn.gelu(proj(SEG_ZU, GMLP_WIDTH))
    vn = _rms(jax.nn.gelu(proj(SEG_ZV, GMLP_WIDTH)), gv_ref[...]).astype(BF16)
    group = lax.broadcasted_iota(I32, (CHUNK, GMLP_WIDTH), 1) // GMLP_GROUP_DIM
    for c in range(t // CHUNK):
        vc = vn[c * CHUNK:(c + 1) * CHUNK, :]
        stack = jnp.concatenate(
            [jnp.where(group == g, vc, jnp.zeros_like(vc)) for g in range(GMLP_GROUPS)], axis=0)
        mixed = _dot(wsp_ref[...], stack) + bsp_ref[...]
        yc_ref[c * CHUNK:(c + 1) * CHUNK, :] = (u[c * CHUNK:(c + 1) * CHUNK, :] * mixed).astype(BF16)

    d = x_ref.shape[1]
    for n in range(N_BRANCH):
        gate_ref[:, n * d:(n + 1) * d] = jax.nn.sigmoid(proj(SEG_GATE + n * d, d)).astype(BF16)


def _proj_call(x, g_pre, w1, rope_c, rope_s, g_v, w_sp, b_sp):
    b, l, d = x.shape
    t = PROJ_TOKENS
    nt = l // t
    tok = lambda w: pl.BlockSpec((None, t, w), lambda bi, ti: (bi, ti, 0))
    tr = lambda w: pl.BlockSpec((None, w, t), lambda bi, ti: (bi, 0, ti))
    rope_spec = pl.BlockSpec((t, LANES), lambda bi, ti: (bi * nt + ti, 0))
    sds = lambda shape, dt: jax.ShapeDtypeStruct(shape, dt)
    return pl.pallas_call(
        _proj_kernel,
        grid=(b, nt),
        in_specs=[tok(d), _const_spec(g_pre.shape), _const_spec(w1.shape), rope_spec, rope_spec,
                  _const_spec(g_v.shape), _const_spec(w_sp.shape), _const_spec(b_sp.shape)],
        out_specs=[tr(ATTN_WIDTH), tok(ATTN_WIDTH), tr(ATTN_WIDTH), tr(IDX_HEADS * IDX_DIM),
                   tok(LANES), tr(IDX_HEADS), tok(LRU_WIDTH), tok(LRU_WIDTH), tok(GMLP_WIDTH),
                   tok(N_BRANCH * d)],
        out_shape=[sds((b, ATTN_WIDTH, l), BF16), sds((b, l, ATTN_WIDTH), BF16),
                   sds((b, ATTN_WIDTH, l), BF16), sds((b, IDX_HEADS * IDX_DIM, l), BF16),
                   sds((b, l, LANES), BF16), sds((b, IDX_HEADS, l), F32),
                   sds((b, l, LRU_WIDTH), F32), sds((b, l, LRU_WIDTH), F32),
                   sds((b, l, GMLP_WIDTH), BF16), sds((b, l, N_BRANCH * d), BF16)],
        compiler_params=_params(2),
        name="in_proj",
    )(x, g_pre, w1, rope_c, rope_s, g_v, w_sp, b_sp)


def _dsa_kernel(qit_ref, wit_ref, qt_ref, ki_ref, k_ref, vt_ref, o_ref,
                key_ref, hi_ref, lo_ref, s_ref, ot_ref, *, top_k):
    r = qt_ref.shape[1]
    l_keys = k_ref.shape[0]
    ck = DSA_KEYS
    sub = DSA_SUB
    nsub = ck // sub
    col0 = pl.program_id(1) * r
    nk = (col0 + r + ck - 1) // ck
    idx_scale = (IDX_DIM ** -0.5) * (IDX_HEADS ** -0.5)

    slab_row = lax.broadcasted_iota(I32, (LANES, r), 0)

    def head_weights(ref, h):
        slab = ref[(h // 2) * LANES:(h // 2 + 1) * LANES, :]
        keep = (slab_row < HEAD_DIM) if h % 2 == 0 else (slab_row >= HEAD_DIM)
        return jnp.where(keep, slab, jnp.zeros_like(slab))

    def chunk_loop(body, init):
        def wrapped(kc, carry):
            return body(pl.multiple_of(kc * ck, ck), carry)
        return lax.fori_loop(0, nk, wrapped, init)

    wit = wit_ref[...]
    w_idx = [head_weights(qit_ref, h) for h in range(IDX_HEADS)]
    qpos = col0 + lax.broadcasted_iota(I32, (sub, r), 1)
    krow = lax.broadcasted_iota(I32, (sub, r), 0)

    def score_chunk(k0, kmax):
        for sb in range(nsub):
            k1 = k0 + sb * sub
            ki = ki_ref[pl.ds(k1, sub), :]
            acc = jnp.zeros((sub, r), F32)
            for h in range(IDX_HEADS):
                acc = acc + wit[h:h + 1, :] * jnp.maximum(_dot(ki, w_idx[h]), 0.0)
            bits = pltpu.bitcast(acc * idx_scale, I32)
            key = bits ^ ((bits >> 31) & INT_MAX)
            key = jnp.where(k1 + krow <= qpos, key, INT_MIN)
            key_ref[pl.ds(k1, sub), :] = key
            hi_ref[pl.ds(k1, sub), :] = (key >> 16).astype(I16)
            lo_ref[pl.ds(k1, sub), :] = ((key & 0xFFFF) - 32768).astype(I16)
            for j in range(sub // SUBLANES):
                kmax = jnp.maximum(kmax, key[j * SUBLANES:(j + 1) * SUBLANES, :])
        return kmax

    kmax = chunk_loop(score_chunk, jnp.full((SUBLANES, r), INT_MIN, I32))
    top_hi = jnp.max(kmax, axis=0, keepdims=True) >> 16

    n_acc = 4

    def count(ref, rows, pred):
        dt = ref.dtype
        one, zero = jnp.ones((), dt), jnp.zeros((), dt)

        def body(k0, accs):
            accs = list(accs)
            chunk = ref[pl.ds(k0, ck), :]
            for j in range(ck // rows):
                blk = chunk[j * rows:(j + 1) * rows, :]
                accs[j % n_acc] = accs[j % n_acc] + jnp.where(pred(blk, k0 + j * rows), one, zero)
            return tuple(accs)

        accs = chunk_loop(body, tuple(jnp.zeros((rows, r), dt) for _ in range(n_acc)))
        tot = accs[0].astype(I32)
        for a in accs[1:]:
            tot = tot + a.astype(I32)
        return jnp.sum(tot, axis=0, keepdims=True)

    def bisect16(ref, need, start=None, n_bits=16):
        rows = 2 * SUBLANES

        def step(b, lo):
            cand = lo + lax.shift_left(jnp.int32(1), n_bits - 1 - b)
            cand16 = jnp.broadcast_to(cand.astype(I16), (rows, r))
            n_ge = count(ref, rows, lambda blk, _: blk >= cand16)
            return jnp.where(n_ge >= need, cand, lo)

        if start is None:
            start = jnp.full((1, r), -32768, I32)
        return lax.fori_loop(0, n_bits, step, start)

    window_bits = 9
    floor_hi = jnp.maximum(top_hi - (2 ** window_bits - 1), -32768)
    floor16 = jnp.broadcast_to(floor_hi.astype(I16), (2 * SUBLANES, r))
    n_window = count(hi_ref, 2 * SUBLANES, lambda blk, _: blk >= floor16)
    window_ok = jnp.min(jnp.where(n_window >= top_k, 1.0, 0.0)) > 0.0
    thr_hi = lax.cond(window_ok,
                      lambda: bisect16(hi_ref, top_k, floor_hi, window_bits),
                      lambda: bisect16(hi_ref, top_k))
    thr_hi16 = jnp.broadcast_to(thr_hi.astype(I16), (2 * SUBLANES, r))
    n_hi_gt = count(hi_ref, 2 * SUBLANES, lambda blk, _: blk > thr_hi16)

    def mask_low(k0, carry):
        sl = pl.ds(k0, ck)
        match = hi_ref[sl, :] == jnp.broadcast_to(thr_hi.astype(I16), (ck, r))
        lo_ref[sl, :] = jnp.where(match, lo_ref[sl, :], jnp.int16(-32768))
        return carry

    chunk_loop(mask_low, 0)
    thr_lo = bisect16(lo_ref, top_k - n_hi_gt)
    thr = thr_hi * 65536 + (thr_lo + 32768)

    thr8 = jnp.broadcast_to(thr, (SUBLANES, r))
    n_gt = count(key_ref, SUBLANES, lambda blk, _: blk > thr8)
    n_eq = count(key_ref, SUBLANES, lambda blk, _: blk == thr8)
    need = top_k - n_gt
    real = thr > INT_MIN
    has_tie = jnp.logical_and(real, n_eq > need)
    row8 = lax.broadcasted_iota(I32, (SUBLANES, r), 0)

    def tie_search():
        n_bits = max(1, int(l_keys - 1).bit_length())

        def pos_step(b, last):
            cand = last + lax.shift_left(jnp.int32(1), n_bits - 1 - b)
            n_before = count(key_ref, SUBLANES,
                             lambda blk, p0: jnp.logical_and(blk == thr8, p0 + row8 < cand))
            return jnp.where(n_before < need, cand, last)

        return lax.fori_loop(0, n_bits, pos_step, jnp.zeros((1, r), I32))

    any_tie = jnp.max(jnp.where(has_tie, 1.0, 0.0)) > 0.0
    key_row = lax.broadcasted_iota(I32, (ck, r), 0)

    def write_bias(bias_of):
        def bias_chunk(k0, carry):
            sl = pl.ds(k0, ck)
            key_ref[sl, :] = pltpu.bitcast(bias_of(key_ref[sl, :], k0), I32)
            return carry
        chunk_loop(bias_chunk, 0)

    def keep(selected):
        return jnp.where(selected, 0.0, MASK_BIAS).astype(F32)

    @pl.when(jnp.logical_not(any_tie))
    def _():
        thr_b = jnp.broadcast_to(jnp.where(real, thr, INT_MIN + 1), (ck, r))
        write_bias(lambda key, k0: keep(key >= thr_b))

    @pl.when(any_tie)
    def _():
        last = jnp.where(has_tie, tie_search(), jnp.where(real, INT_MAX, -1))
        thr_b = jnp.broadcast_to(thr, (ck, r))
        last_b = jnp.broadcast_to(last, (ck, r))
        write_bias(lambda key, k0: jnp.where(key == thr_b, keep(k0 + key_row <= last_b),
                                             keep(key > thr_b)))

    w_q = [head_weights(qt_ref, h) for h in range(N_HEADS)]
    pack = 2 * SUBLANES

    def fold8(a):
        out = a[0:SUBLANES, :]
        for j in range(1, a.shape[0] // SUBLANES):
            out = jnp.maximum(out, a[j * SUBLANES:(j + 1) * SUBLANES, :])
        return out

    def logits_chunk(k0, ms):
        ms = list(ms)
        for sb in range(nsub):
            k1 = k0 + sb * sub
            bias = pltpu.bitcast(key_ref[pl.ds(k1, sub), :], F32)
            for h in range(N_HEADS):
                pair = (h // 2) * LANES
                s = _dot(k_ref[pl.ds(k1, sub), pair:pair + LANES], w_q[h]) + bias
                s_ref[h, pl.ds(k1, sub), :] = s.astype(BF16)
                ms[h] = jnp.maximum(ms[h], fold8(s))
        return tuple(ms)

    ms = chunk_loop(logits_chunk,
                    tuple(jnp.full((SUBLANES, r), -jnp.inf, F32) for _ in range(N_HEADS)))
    ms = [jnp.broadcast_to(jnp.max(m, axis=0, keepdims=True).astype(BF16), (ck, r)) for m in ms]

    ot_ref[...] = jnp.zeros_like(ot_ref)
    ones_rows = jnp.ones((pack, ck), BF16)

    def pv_chunk(k0, carry):
        sl = pl.ds(k0, ck)
        for h in range(N_HEADS):
            p = jnp.exp2(s_ref[h, sl, :] - ms[h])
            lhs = jnp.concatenate([vt_ref[h * HEAD_DIM:(h + 1) * HEAD_DIM, sl], ones_rows], axis=0)
            ot_ref[h] += _dot(lhs, p)
        return carry

    chunk_loop(pv_chunk, 0)
    outs = []
    for h in range(N_HEADS):
        acc = ot_ref[h]
        outs.append(acc[0:HEAD_DIM, :] * (1.0 / acc[HEAD_DIM:HEAD_DIM + 1, :]))
    o_ref[...] = jnp.concatenate(outs, axis=0).T.astype(BF16)


def _dsa_call(qit, wit, qt, ki, k, vt):
    b, l, _ = k.shape
    r = DSA_ROWS
    top_k = min(INDEX_TOPK, l // 4)
    col = lambda h: pl.BlockSpec((None, h, r), lambda bi, ti: (bi, 0, ti))
    full = lambda h, w: pl.BlockSpec((None, h, w), lambda bi, ti: (bi, 0, 0),
                                     pipeline_mode=pl.Buffered(1))
    return pl.pallas_call(
        functools.partial(_dsa_kernel, top_k=top_k),
        grid=(b, l // r),
        in_specs=[col(IDX_HEADS * IDX_DIM), col(IDX_HEADS), col(ATTN_WIDTH),
                  full(l, LANES), full(l, ATTN_WIDTH), full(ATTN_WIDTH, l)],
        out_specs=pl.BlockSpec((None, r, ATTN_WIDTH), lambda bi, ti: (bi, ti, 0)),
        out_shape=jax.ShapeDtypeStruct((b, l, ATTN_WIDTH), BF16),
        scratch_shapes=[pltpu.VMEM((l, r), I32), pltpu.VMEM((l, r), I16), pltpu.VMEM((l, r), I16),
                        pltpu.VMEM((N_HEADS, l, r), BF16),
                        pltpu.VMEM((N_HEADS, HEAD_DIM + 2 * SUBLANES, r), F32)],
        compiler_params=_params(2),
        name="dsa_attention",
    )(qit, wit, qt, ki, k, vt)


def _lru_kernel(xr_ref, gr_ref, cw_ref, cb_ref, wa_ref, ba_ref, wx_ref, bx_ref, lam_ref,
                y_ref, hist_ref, h_ref):
    t = xr_ref.shape[0]

    @pl.when(pl.program_id(1) == 0)
    def _():
        hist_ref[...] = jnp.zeros_like(hist_ref)
        h_ref[...] = jnp.zeros_like(h_ref)

    x = xr_ref[...]
    hist = hist_ref[...]
    row = lax.broadcasted_iota(I32, x.shape, 0)
    row8 = lax.broadcasted_iota(I32, hist.shape, 0)
    cw = cw_ref[...]
    conv = x * cw[CONV_WIDTH - 1:CONV_WIDTH, :] + cb_ref[...]
    for d in range(1, CONV_WIDTH):
        shifted = pltpu.roll(x, d, 0)
        head = jnp.where(row8 < d, pltpu.roll(hist, d, 0), shifted[0:SUBLANES, :])
        shifted = jnp.concatenate([head, shifted[SUBLANES:, :]], axis=0)
        conv = conv + shifted * cw[CONV_WIDTH - 1 - d:CONV_WIDTH - d, :]
    hist_ref[...] = x[t - SUBLANES:, :]

    cb16 = conv.astype(BF16)
    rg = jax.nn.sigmoid(_dot(cb16, wa_ref[...]) + ba_ref[...])
    ig = jax.nn.sigmoid(_dot(cb16, wx_ref[...]) + bx_ref[...])
    log_a = -LRU_C * rg * jax.nn.softplus(-lam_ref[...])
    a = jnp.exp(log_a)
    inp = jnp.sqrt(-jnp.tanh(log_a) * (a * a + 1.0)) * (ig * conv)

    in_group = row & (SUBLANES - 1)
    d = 1
    while d < SUBLANES:
        ok = in_group >= d
        inp = jnp.where(ok, a * pltpu.roll(inp, d, 0) + inp, inp)
        a = jnp.where(ok, a * pltpu.roll(a, d, 0), a)
        d *= 2
    carry = h_ref[0:1, :]
    gelu_g = jax.nn.gelu(gr_ref[...])
    pack = 2 * SUBLANES
    for g in range(t // pack):
        halves = []
        for half in range(2):
            rows = slice(g * pack + half * SUBLANES, g * pack + (half + 1) * SUBLANES)
            hg = inp[rows, :] + a[rows, :] * carry
            carry = hg[SUBLANES - 1:SUBLANES, :]
            halves.append(hg * gelu_g[rows, :])
        y_ref[g * pack:(g + 1) * pack, :] = jnp.concatenate(halves, axis=0).astype(BF16)
    h_ref[...] = jnp.broadcast_to(carry, h_ref.shape)


def _lru_call(xr, gr, conv_w, conv_b, wa, ba, wx, bx, lam):
    b, l, w = xr.shape
    t = LRU_TOKENS
    tok = pl.BlockSpec((None, t, w), lambda bi, ti: (bi, ti, 0))
    consts = [conv_w, conv_b, wa, ba, wx, bx, lam]
    return pl.pallas_call(
        _lru_kernel,
        grid=(b, l // t),
        in_specs=[tok, tok] + [_const_spec(c.shape) for c in consts],
        out_specs=tok,
        out_shape=jax.ShapeDtypeStruct((b, l, w), BF16),
        scratch_shapes=[pltpu.VMEM((SUBLANES, w), F32), pltpu.VMEM((SUBLANES, w), F32)],
        compiler_params=_params(2),
        name="conv_rglru",
    )(xr, gr, *consts)


def _post_kernel(x_ref, ya_ref, yb_ref, yc_ref, gate_ref, p_ref,
                 wb_ref, wo_ref, gpm_ref, gpf_ref, wup_ref, wdn_ref, gpo_ref,
                 wple_ref, wpg_ref, gpp_ref, o_ref):
    d = x_ref.shape[1]
    w = ya_ref.shape[1]
    x = x_ref[...]
    merged = jnp.zeros(x.shape, F32)
    for n, y_ref in enumerate((ya_ref, yb_ref, yc_ref)):
        branch = _dot(y_ref[...], wb_ref[n * w:(n + 1) * w, :])
        merged = merged + gate_ref[:, n * d:(n + 1) * d].astype(F32) * branch
    x = x + _rms(_dot(merged.astype(BF16), wo_ref[...]), gpm_ref[...])

    h2 = _rms(x, gpf_ref[...]).astype(BF16)
    f = jnp.zeros(x.shape, F32)
    for c in range(wup_ref.shape[1] // FFN_CHUNK):
        sl = slice(c * FFN_CHUNK, (c + 1) * FFN_CHUNK)
        hid = jnp.square(jnp.maximum(_dot(h2, wup_ref[:, sl]), 0.0)).astype(BF16)
        f = f + _dot(hid, wdn_ref[sl, :])
    x = x + _rms(f, gpo_ref[...])

    ple = _dot(p_ref[...].astype(BF16), wple_ref[...]) * jax.nn.sigmoid(
        _dot(x.astype(BF16), wpg_ref[...]))
    o_ref[...] = x + _rms(ple, gpp_ref[...])


def _post_call(x, ya, yb, yc, gates, p, layer, consts):
    b, l, d = x.shape
    t = POST_TOKENS
    tok = lambda w: pl.BlockSpec((None, t, w), lambda bi, ti: (bi, ti, 0))
    p_spec = pl.BlockSpec((None, None, t, p.shape[3]), lambda bi, ti: (layer, bi, ti, 0))
    return pl.pallas_call(
        _post_kernel,
        grid=(b, l // t),
        in_specs=[tok(d), tok(ya.shape[2]), tok(yb.shape[2]), tok(yc.shape[2]),
                  tok(gates.shape[2]), p_spec] + [_const_spec(c.shape) for c in consts],
        out_specs=tok(d),
        out_shape=jax.ShapeDtypeStruct((b, l, d), F32),
        compiler_params=_params(2),
        name="merge_ffn_ple",
    )(x, ya, yb, yc, gates, p, *consts)


def _block_diag(w):
    nb, bi, bo = w.shape
    eye = jnp.eye(nb, dtype=w.dtype)
    return jnp.einsum("hij,hg->higj", w, eye).reshape(nb * bi, nb * bo)


def _pack_w_in(w_in):
    w_in = w_in.astype(BF16)
    d = w_in.shape[0]
    kiwi = w_in[:, SEG_KIWI:SEG_KIWI + IDX_DIM + IDX_HEADS]
    pad = jnp.zeros((d, LANES - IDX_DIM - IDX_HEADS), w_in.dtype)
    rest = w_in[:, SEG_KIWI + IDX_DIM + IDX_HEADS:]
    return jnp.concatenate([w_in[:, :SEG_KIWI], kiwi, pad, rest], axis=1)


def kernel(x, p, positions, g_pre_mix, w_in, conv_w, conv_b, w_rg_a, b_rg_a, w_rg_x, b_rg_x,
           lru_lambda, g_gmlp_v, w_spatial, b_spatial, w_branch, w_out, g_post_mix,
           g_pre_ffn, w_ffn_up, w_ffn_down, g_post_ffn, w_ple, w_ple_gate, g_post_ple):
    depth = w_in.shape[0]
    b, l, d = x.shape
    assert l % PROJ_TOKENS == 0 and l % DSA_KEYS == 0 and l % LRU_TOKENS == 0 and l % DSA_ROWS == 0
    assert l % POST_TOKENS == 0 and (b * l) % 1024 == 0 and DSA_KEYS >= INDEX_TOPK
    row = lambda a: a.reshape(1, -1)

    rope_c, rope_s = _rope_tables(positions)
    tril = jnp.tril(jnp.ones((CHUNK, CHUNK), dtype=bool))

    for i in range(depth):
        w_sp = jnp.where(tril[None], w_spatial[i], 0)
        w_sp = jnp.transpose(w_sp, (1, 0, 2)).reshape(CHUNK, GMLP_GROUPS * CHUNK).astype(BF16)
        b_sp = jnp.repeat(jnp.transpose(b_spatial[i]), GMLP_GROUP_DIM, axis=1)

        qt, k, vt, qit, ki, wit, xr, gr, yc, gates = _proj_call(
            x, row(g_pre_mix[i]), _pack_w_in(w_in[i]), rope_c, rope_s,
            row(g_gmlp_v[i]), w_sp, b_sp)

        ya = _dsa_call(qit, wit, qt, ki, k, vt)

        yb = _lru_call(xr, gr, conv_w[i], row(conv_b[i]),
                       _block_diag(w_rg_a[i]).astype(BF16), row(b_rg_a[i]),
                       _block_diag(w_rg_x[i]).astype(BF16), row(b_rg_x[i]),
                       row(lru_lambda[i]))

        consts = [w_branch[i].reshape(N_BRANCH * w_branch.shape[2], d).astype(BF16),
                  w_out[i].astype(BF16), row(g_post_mix[i]), row(g_pre_ffn[i]),
                  w_ffn_up[i].astype(BF16), w_ffn_down[i].astype(BF16), row(g_post_ffn[i]),
                  w_ple[i].astype(BF16), w_ple_gate[i].astype(BF16), row(g_post_ple[i])]
        x = _post_call(x, ya, yb, yc, gates, p, i, consts)
    return x
```

```python
import functools

import jax
import jax.numpy as jnp
import numpy as np
from jax import lax
from jax.experimental import pallas as pl
from jax.experimental.pallas import tpu as pltpu

F32 = jnp.float32
BF16 = jnp.bfloat16
I32 = jnp.int32
I16 = jnp.int16

N_HEADS = 8
HEAD_DIM = 64
ATTN_WIDTH = N_HEADS * HEAD_DIM
ROPE_DIM = HEAD_DIM // 4
ROPE_THETA = 500000.0
IDX_HEADS = 8
IDX_DIM = 64
INDEX_TOPK = 256
LRU_WIDTH = 512
LRU_BLOCKS = 8
CONV_WIDTH = 4
LRU_C = 8.0
GMLP_WIDTH = 512
GMLP_GROUPS = 8
GMLP_GROUP_DIM = GMLP_WIDTH // GMLP_GROUPS
CHUNK = 128
N_BRANCH = 3
EPS = 1e-6

LANES = 128
SUBLANES = 8
VMEM_LIMIT_BYTES = 56 * 1024 * 1024

INT_MIN = np.int32(-(2**31))
INT_MAX = np.int32(2**31 - 1)
MASK_BIAS = -1e30
Q_SCALE = float(np.log2(np.e)) * HEAD_DIM ** -0.5

PROJ_TOKENS = 512
DSA_ROWS = 256
DSA_KEYS = 512
DSA_SUB = 128
LRU_TOKENS = 512
POST_TOKENS = 512
FFN_CHUNK = 1024

SEG_Q, SEG_K, SEG_V, SEG_QI = 0, 512, 1024, 1536
SEG_KIWI = 2048
SEG_XR, SEG_GR, SEG_ZU, SEG_ZV = 2176, 2688, 3200, 3712
SEG_GATE = 4224


def _rms(x, g):
    return x * lax.rsqrt(jnp.mean(x * x, axis=-1, keepdims=True) + EPS) * g


def _dot(a, b):
    return jnp.dot(a, b, preferred_element_type=F32)


def _params(n_grid):
    return pltpu.CompilerParams(
        dimension_semantics=("arbitrary",) * n_grid,
        vmem_limit_bytes=VMEM_LIMIT_BYTES)


def _const_spec(shape):
    zeros = (0,) * len(shape)
    return pl.BlockSpec(shape, lambda *_: zeros, pipeline_mode=pl.Buffered(1))


def _rope_table_kernel(pos_ref, invf_ref, c_ref, s_ref):
    ang = pos_ref[...].astype(F32) * invf_ref[...]
    lane = lax.broadcasted_iota(I32, ang.shape, 1) & (HEAD_DIM - 1)
    cos = jnp.cos(ang)
    sin = jnp.sin(ang)
    half = ROPE_DIM // 2
    c_ref[...] = jnp.where(lane < ROPE_DIM, cos, 1.0)
    s_ref[...] = jnp.where(lane < half, -sin, jnp.where(lane < ROPE_DIM, sin, 0.0))


def _rope_tables(positions):
    n = positions.size
    tile = 1024
    half = ROPE_DIM // 2
    inv_freq = ROPE_THETA ** (-jnp.arange(half, dtype=F32) * 2.0 / ROPE_DIM)
    inv_lane = jnp.tile(inv_freq, LANES // half).reshape(1, LANES)
    out = jax.ShapeDtypeStruct((n, LANES), F32)
    return pl.pallas_call(
        _rope_table_kernel,
        grid=(n // tile,),
        in_specs=[pl.BlockSpec((tile, 1), lambda i: (i, 0)),
                  pl.BlockSpec((1, LANES), lambda i: (0, 0))],
        out_specs=[pl.BlockSpec((tile, LANES), lambda i: (i, 0))] * 2,
        out_shape=[out, out],
        compiler_params=_params(1),
        name="rope_tables",
    )(positions.reshape(n, 1), inv_lane)


def _proj_kernel(x_ref, g_ref, w_ref, c_ref, s_ref, gv_ref, wsp_ref, bsp_ref,
                 qt_ref, k_ref, vt_ref, qit_ref, ki_ref, wit_ref, xr_ref, gr_ref,
                 yc_ref, gate_ref):
    t = x_ref.shape[0]
    h = _rms(x_ref[...], g_ref[...]).astype(BF16)
    cos = c_ref[...]
    sin = s_ref[...]
    lane = lax.broadcasted_iota(I32, (t, LANES), 1)
    first_half = (lane & (HEAD_DIM - 1)) < (ROPE_DIM // 2)

    def proj(start, width):
        return _dot(h, w_ref[:, start:start + width])

    def rope(p):
        cols = []
        for j in range(p.shape[1] // LANES):
            pj = p[:, j * LANES:(j + 1) * LANES]
            partner = jnp.where(first_half,
                                pltpu.roll(pj, LANES - ROPE_DIM // 2, 1),
                                pltpu.roll(pj, ROPE_DIM // 2, 1))
            cols.append(pj * cos + partner * sin)
        return cols[0] if len(cols) == 1 else jnp.concatenate(cols, axis=1)

    qt_ref[...] = (rope(proj(SEG_Q, ATTN_WIDTH)) * Q_SCALE).T.astype(BF16)
    k_ref[...] = rope(proj(SEG_K, ATTN_WIDTH)).astype(BF16)
    vt_ref[...] = proj(SEG_V, ATTN_WIDTH).T.astype(BF16)
    qit_ref[...] = rope(proj(SEG_QI, IDX_HEADS * IDX_DIM)).T.astype(BF16)

    kiwi = proj(SEG_KIWI, LANES)
    kiwi = jnp.where(lane < IDX_DIM, rope(kiwi), kiwi)
    ki_ref[...] = jnp.where(lane < IDX_DIM, kiwi, pltpu.roll(kiwi, IDX_DIM, 1)).astype(BF16)
    wit_ref[...] = kiwi.T[IDX_DIM:IDX_DIM + IDX_HEADS, :]

    xr_ref[...] = proj(SEG_XR, LRU_WIDTH)
    gr_ref[...] = proj(SEG_GR, LRU_WIDTH)

    u = jax.nn.gelu(proj(SEG_ZU, GMLP_WIDTH))
    vn = _rms(jax.nn.gelu(proj(SEG_ZV, GMLP_WIDTH)), gv_ref[...]).astype(BF16)
    group = lax.broadcasted_iota(I32, (CHUNK, GMLP_WIDTH), 1) // GMLP_GROUP_DIM
    for c in range(t // CHUNK):
        vc = vn[c * CHUNK:(c + 1) * CHUNK, :]
        stack = jnp.concatenate(
            [jnp.where(group == g, vc, jnp.zeros_like(vc)) for g in range(GMLP_GROUPS)], axis=0)
        mixed = _dot(wsp_ref[...], stack) + bsp_ref[...]
        yc_ref[c * CHUNK:(c + 1) * CHUNK, :] = (u[c * CHUNK:(c + 1) * CHUNK, :] * mixed).astype(BF16)

    d = x_ref.shape[1]
    for n in range(N_BRANCH):
        gate_ref[:, n * d:(n + 1) * d] = jax.nn.sigmoid(proj(SEG_GATE + n * d, d)).astype(BF16)


def _proj_call(x, g_pre, w1, rope_c, rope_s, g_v, w_sp, b_sp):
    b, l, d = x.shape
    t = PROJ_TOKENS
    nt = l // t
    tok = lambda w: pl.BlockSpec((None, t, w), lambda bi, ti: (bi, ti, 0))
    tr = lambda w: pl.BlockSpec((None, w, t), lambda bi, ti: (bi, 0, ti))
    rope_spec = pl.BlockSpec((t, LANES), lambda bi, ti: (bi * nt + ti, 0))
    sds = lambda shape, dt: jax.ShapeDtypeStruct(shape, dt)
    return pl.pallas_call(
        _proj_kernel,
        grid=(b, nt),
        in_specs=[tok(d), _const_spec(g_pre.shape), _const_spec(w1.shape), rope_spec, rope_spec,
                  _const_spec(g_v.shape), _const_spec(w_sp.shape), _const_spec(b_sp.shape)],
        out_specs=[tr(ATTN_WIDTH), tok(ATTN_WIDTH), tr(ATTN_WIDTH), tr(IDX_HEADS * IDX_DIM),
                   tok(LANES), tr(IDX_HEADS), tok(LRU_WIDTH), tok(LRU_WIDTH), tok(GMLP_WIDTH),
                   tok(N_BRANCH * d)],
        out_shape=[sds((b, ATTN_WIDTH, l), BF16), sds((b, l, ATTN_WIDTH), BF16),
                   sds((b, ATTN_WIDTH, l), BF16), sds((b, IDX_HEADS * IDX_DIM, l), BF16),
                   sds((b, l, LANES), BF16), sds((b, IDX_HEADS, l), F32),
                   sds((b, l, LRU_WIDTH), F32), sds((b, l, LRU_WIDTH), F32),
                   sds((b, l, GMLP_WIDTH), BF16), sds((b, l, N_BRANCH * d), BF16)],
        compiler_params=_params(2),
        name="in_proj",
    )(x, g_pre, w1, rope_c, rope_s, g_v, w_sp, b_sp)


def _dsa_kernel(qit_ref, wit_ref, qt_ref, ki_ref, k_ref, vt_ref, o_ref,
                key_ref, hi_ref, lo_ref, s_ref, ot_ref, *, top_k):
    r = qt_ref.shape[1]
    l_keys = k_ref.shape[0]
    ck = DSA_KEYS
    sub = DSA_SUB
    nsub = ck // sub
    col0 = pl.program_id(1) * r
    nk = (col0 + r + ck - 1) // ck
    idx_scale = (IDX_DIM ** -0.5) * (IDX_HEADS ** -0.5)

    slab_row = lax.broadcasted_iota(I32, (LANES, r), 0)

    def head_weights(ref, h):
        slab = ref[(h // 2) * LANES:(h // 2 + 1) * LANES, :]
        keep = (slab_row < HEAD_DIM) if h % 2 == 0 else (slab_row >= HEAD_DIM)
        return jnp.where(keep, slab, jnp.zeros_like(slab))

    def chunk_loop(body, init):
        def wrapped(kc, carry):
            return body(pl.multiple_of(kc * ck, ck), carry)
        return lax.fori_loop(0, nk, wrapped, init)

    wit = wit_ref[...]
    w_idx = [head_weights(qit_ref, h) for h in range(IDX_HEADS)]
    qpos = col0 + lax.broadcasted_iota(I32, (sub, r), 1)
    krow = lax.broadcasted_iota(I32, (sub, r), 0)

    def score_chunk(k0, carry):
        for sb in range(nsub):
            k1 = k0 + sb * sub
            ki = ki_ref[pl.ds(k1, sub), :]
            acc = jnp.zeros((sub, r), F32)
            for h in range(IDX_HEADS):
                acc = acc + wit[h:h + 1, :] * jnp.maximum(_dot(ki, w_idx[h]), 0.0)
            bits = pltpu.bitcast(acc * idx_scale, I32)
            key = bits ^ ((bits >> 31) & INT_MAX)
            key = jnp.where(k1 + krow <= qpos, key, INT_MIN)
            key_ref[pl.ds(k1, sub), :] = key
            hi_ref[pl.ds(k1, sub), :] = (key >> 16).astype(I16)
            lo_ref[pl.ds(k1, sub), :] = ((key & 0xFFFF) - 32768).astype(I16)
        return carry

    chunk_loop(score_chunk, 0)

    n_acc = 4

    def count(ref, rows, pred):
        dt = ref.dtype
        one, zero = jnp.ones((), dt), jnp.zeros((), dt)

        def body(k0, accs):
            accs = list(accs)
            chunk = ref[pl.ds(k0, ck), :]
            for j in range(ck // rows):
                blk = chunk[j * rows:(j + 1) * rows, :]
                accs[j % n_acc] = accs[j % n_acc] + jnp.where(pred(blk, k0 + j * rows), one, zero)
            return tuple(accs)

        accs = chunk_loop(body, tuple(jnp.zeros((rows, r), dt) for _ in range(n_acc)))
        tot = accs[0].astype(I32)
        for a in accs[1:]:
            tot = tot + a.astype(I32)
        return jnp.sum(tot, axis=0, keepdims=True)

    def bisect16(ref, need):
        rows = 2 * SUBLANES

        def step(b, lo):
            cand = lo + lax.shift_left(jnp.int32(1), 15 - b)
            cand16 = jnp.broadcast_to(cand.astype(I16), (rows, r))
            n_ge = count(ref, rows, lambda blk, _: blk >= cand16)
            return jnp.where(n_ge >= need, cand, lo)

        return lax.fori_loop(0, 16, step, jnp.full((1, r), -32768, I32))

    thr_hi = bisect16(hi_ref, top_k)
    thr_hi16 = jnp.broadcast_to(thr_hi.astype(I16), (2 * SUBLANES, r))
    n_hi_gt = count(hi_ref, 2 * SUBLANES, lambda blk, _: blk > thr_hi16)

    def mask_low(k0, carry):
        sl = pl.ds(k0, ck)
        match = hi_ref[sl, :] == jnp.broadcast_to(thr_hi.astype(I16), (ck, r))
        lo_ref[sl, :] = jnp.where(match, lo_ref[sl, :], jnp.int16(-32768))
        return carry

    chunk_loop(mask_low, 0)
    thr_lo = bisect16(lo_ref, top_k - n_hi_gt)
    thr = thr_hi * 65536 + (thr_lo + 32768)

    thr8 = jnp.broadcast_to(thr, (SUBLANES, r))
    n_gt = count(key_ref, SUBLANES, lambda blk, _: blk > thr8)
    n_eq = count(key_ref, SUBLANES, lambda blk, _: blk == thr8)
    need = top_k - n_gt
    real = thr > INT_MIN
    has_tie = jnp.logical_and(real, n_eq > need)
    row8 = lax.broadcasted_iota(I32, (SUBLANES, r), 0)

    def tie_search():
        n_bits = max(1, int(l_keys - 1).bit_length())

        def pos_step(b, last):
            cand = last + lax.shift_left(jnp.int32(1), n_bits - 1 - b)
            n_before = count(key_ref, SUBLANES,
                             lambda blk, p0: jnp.logical_and(blk == thr8, p0 + row8 < cand))
            return jnp.where(n_before < need, cand, last)

        return lax.fori_loop(0, n_bits, pos_step, jnp.zeros((1, r), I32))

    any_tie = jnp.max(jnp.where(has_tie, 1.0, 0.0)) > 0.0
    key_row = lax.broadcasted_iota(I32, (ck, r), 0)

    def write_bias(bias_of):
        def bias_chunk(k0, carry):
            sl = pl.ds(k0, ck)
            key_ref[sl, :] = pltpu.bitcast(bias_of(key_ref[sl, :], k0), I32)
            return carry
        chunk_loop(bias_chunk, 0)

    def keep(selected):
        return jnp.where(selected, 0.0, MASK_BIAS).astype(F32)

    @pl.when(jnp.logical_not(any_tie))
    def _():
        thr_b = jnp.broadcast_to(jnp.where(real, thr, INT_MIN + 1), (ck, r))
        write_bias(lambda key, k0: keep(key >= thr_b))

    @pl.when(any_tie)
    def _():
        last = jnp.where(has_tie, tie_search(), jnp.where(real, INT_MAX, -1))
        thr_b = jnp.broadcast_to(thr, (ck, r))
        last_b = jnp.broadcast_to(last, (ck, r))
        write_bias(lambda key, k0: jnp.where(key == thr_b, keep(k0 + key_row <= last_b),
                                             keep(key > thr_b)))

    w_q = [head_weights(qt_ref, h) for h in range(N_HEADS)]
    pack = 2 * SUBLANES

    def fold8(a):
        out = a[0:SUBLANES, :]
        for j in range(1, a.shape[0] // SUBLANES):
            out = jnp.maximum(out, a[j * SUBLANES:(j + 1) * SUBLANES, :])
        return out

    def logits_chunk(k0, ms):
        ms = list(ms)
        for sb in range(nsub):
            k1 = k0 + sb * sub
            bias = pltpu.bitcast(key_ref[pl.ds(k1, sub), :], F32)
            for h in range(N_HEADS):
                pair = (h // 2) * LANES
                s = _dot(k_ref[pl.ds(k1, sub), pair:pair + LANES], w_q[h]) + bias
                s_ref[h, pl.ds(k1, sub), :] = s.astype(BF16)
                ms[h] = jnp.maximum(ms[h], fold8(s))
        return tuple(ms)

    ms = chunk_loop(logits_chunk,
                    tuple(jnp.full((SUBLANES, r), -jnp.inf, F32) for _ in range(N_HEADS)))
    ms = [jnp.broadcast_to(jnp.max(m, axis=0, keepdims=True).astype(BF16), (ck, r)) for m in ms]

    ot_ref[...] = jnp.zeros_like(ot_ref)
    ones_rows = jnp.ones((pack, ck), BF16)

    def pv_chunk(k0, carry):
        sl = pl.ds(k0, ck)
        for h in range(N_HEADS):
            p = jnp.exp2(s_ref[h, sl, :] - ms[h])
            lhs = jnp.concatenate([vt_ref[h * HEAD_DIM:(h + 1) * HEAD_DIM, sl], ones_rows], axis=0)
            ot_ref[h] += _dot(lhs, p)
        return carry

    chunk_loop(pv_chunk, 0)
    outs = []
    for h in range(N_HEADS):
        acc = ot_ref[h]
        outs.append(acc[0:HEAD_DIM, :] * (1.0 / acc[HEAD_DIM:HEAD_DIM + 1, :]))
    o_ref[...] = jnp.concatenate(outs, axis=0).T.astype(BF16)


def _dsa_call(qit, wit, qt, ki, k, vt):
    b, l, _ = k.shape
    r = DSA_ROWS
    top_k = min(INDEX_TOPK, l // 4)
    col = lambda h: pl.BlockSpec((None, h, r), lambda bi, ti: (bi, 0, ti))
    full = lambda h, w: pl.BlockSpec((None, h, w), lambda bi, ti: (bi, 0, 0),
                                     pipeline_mode=pl.Buffered(1))
    return pl.pallas_call(
        functools.partial(_dsa_kernel, top_k=top_k),
        grid=(b, l // r),
        in_specs=[col(IDX_HEADS * IDX_DIM), col(IDX_HEADS), col(ATTN_WIDTH),
                  full(l, LANES), full(l, ATTN_WIDTH), full(ATTN_WIDTH, l)],
        out_specs=pl.BlockSpec((None, r, ATTN_WIDTH), lambda bi, ti: (bi, ti, 0)),
        out_shape=jax.ShapeDtypeStruct((b, l, ATTN_WIDTH), BF16),
        scratch_shapes=[pltpu.VMEM((l, r), I32), pltpu.VMEM((l, r), I16), pltpu.VMEM((l, r), I16),
                        pltpu.VMEM((N_HEADS, l, r), BF16),
                        pltpu.VMEM((N_HEADS, HEAD_DIM + 2 * SUBLANES, r), F32)],
        compiler_params=_params(2),
        name="dsa_attention",
    )(qit, wit, qt, ki, k, vt)


def _lru_kernel(xr_ref, gr_ref, cw_ref, cb_ref, wa_ref, ba_ref, wx_ref, bx_ref, lam_ref,
                y_ref, hist_ref, h_ref):
    t = xr_ref.shape[0]

    @pl.when(pl.program_id(1) == 0)
    def _():
        hist_ref[...] = jnp.zeros_like(hist_ref)
        h_ref[...] = jnp.zeros_like(h_ref)

    x = xr_ref[...]
    hist = hist_ref[...]
    row = lax.broadcasted_iota(I32, x.shape, 0)
    row8 = lax.broadcasted_iota(I32, hist.shape, 0)
    cw = cw_ref[...]
    conv = x * cw[CONV_WIDTH - 1:CONV_WIDTH, :] + cb_ref[...]
    for d in range(1, CONV_WIDTH):
        shifted = pltpu.roll(x, d, 0)
        head = jnp.where(row8 < d, pltpu.roll(hist, d, 0), shifted[0:SUBLANES, :])
        shifted = jnp.concatenate([head, shifted[SUBLANES:, :]], axis=0)
        conv = conv + shifted * cw[CONV_WIDTH - 1 - d:CONV_WIDTH - d, :]
    hist_ref[...] = x[t - SUBLANES:, :]

    cb16 = conv.astype(BF16)
    rg = jax.nn.sigmoid(_dot(cb16, wa_ref[...]) + ba_ref[...])
    ig = jax.nn.sigmoid(_dot(cb16, wx_ref[...]) + bx_ref[...])
    log_a = -LRU_C * rg * jax.nn.softplus(-lam_ref[...])
    a = jnp.exp(log_a)
    inp = jnp.sqrt(-jnp.tanh(log_a) * (a * a + 1.0)) * (ig * conv)

    in_group = row & (SUBLANES - 1)
    d = 1
    while d < SUBLANES:
        ok = in_group >= d
        inp = jnp.where(ok, a * pltpu.roll(inp, d, 0) + inp, inp)
        a = jnp.where(ok, a * pltpu.roll(a, d, 0), a)
        d *= 2
    carry = h_ref[0:1, :]
    gelu_g = jax.nn.gelu(gr_ref[...])
    pack = 2 * SUBLANES
    for g in range(t // pack):
        halves = []
        for half in range(2):
            rows = slice(g * pack + half * SUBLANES, g * pack + (half + 1) * SUBLANES)
            hg = inp[rows, :] + a[rows, :] * carry
            carry = hg[SUBLANES - 1:SUBLANES, :]
            halves.append(hg * gelu_g[rows, :])
        y_ref[g * pack:(g + 1) * pack, :] = jnp.concatenate(halves, axis=0).astype(BF16)
    h_ref[...] = jnp.broadcast_to(carry, h_ref.shape)


def _lru_call(xr, gr, conv_w, conv_b, wa, ba, wx, bx, lam):
    b, l, w = xr.shape
    t = LRU_TOKENS
    tok = pl.BlockSpec((None, t, w), lambda bi, ti: (bi, ti, 0))
    consts = [conv_w, conv_b, wa, ba, wx, bx, lam]
    return pl.pallas_call(
        _lru_kernel,
        grid=(b, l // t),
        in_specs=[tok, tok] + [_const_spec(c.shape) for c in consts],
        out_specs=tok,
        out_shape=jax.ShapeDtypeStruct((b, l, w), BF16),
        scratch_shapes=[pltpu.VMEM((SUBLANES, w), F32), pltpu.VMEM((SUBLANES, w), F32)],
        compiler_params=_params(2),
        name="conv_rglru",
    )(xr, gr, *consts)


def _post_kernel(x_ref, ya_ref, yb_ref, yc_ref, gate_ref, p_ref,
                 wb_ref, wo_ref, gpm_ref, gpf_ref, wup_ref, wdn_ref, gpo_ref,
                 wple_ref, wpg_ref, gpp_ref, o_ref):
    d = x_ref.shape[1]
    w = ya_ref.shape[1]
    x = x_ref[...]
    merged = jnp.zeros(x.shape, F32)
    for n, y_ref in enumerate((ya_ref, yb_ref, yc_ref)):
        branch = _dot(y_ref[...], wb_ref[n * w:(n + 1) * w, :])
        merged = merged + gate_ref[:, n * d:(n + 1) * d].astype(F32) * branch
    x = x + _rms(_dot(merged.astype(BF16), wo_ref[...]), gpm_ref[...])

    h2 = _rms(x, gpf_ref[...]).astype(BF16)
    f = jnp.zeros(x.shape, F32)
    for c in range(wup_ref.shape[1] // FFN_CHUNK):
        sl = slice(c * FFN_CHUNK, (c + 1) * FFN_CHUNK)
        hid = jnp.square(jnp.maximum(_dot(h2, wup_ref[:, sl]), 0.0)).astype(BF16)
        f = f + _dot(hid, wdn_ref[sl, :])
    x = x + _rms(f, gpo_ref[...])

    ple = _dot(p_ref[...].astype(BF16), wple_ref[...]) * jax.nn.sigmoid(
        _dot(x.astype(BF16), wpg_ref[...]))
    o_ref[...] = x + _rms(ple, gpp_ref[...])


def _post_call(x, ya, yb, yc, gates, p, layer, consts):
    b, l, d = x.shape
    t = POST_TOKENS
    tok = lambda w: pl.BlockSpec((None, t, w), lambda bi, ti: (bi, ti, 0))
    p_spec = pl.BlockSpec((None, None, t, p.shape[3]), lambda bi, ti: (layer, bi, ti, 0))
    return pl.pallas_call(
        _post_kernel,
        grid=(b, l // t),
        in_specs=[tok(d), tok(ya.shape[2]), tok(yb.shape[2]), tok(yc.shape[2]),
                  tok(gates.shape[2]), p_spec] + [_const_spec(c.shape) for c in consts],
        out_specs=tok(d),
        out_shape=jax.ShapeDtypeStruct((b, l, d), F32),
        compiler_params=_params(2),
        name="merge_ffn_ple",
    )(x, ya, yb, yc, gates, p, *consts)


def _block_diag(w):
    nb, bi, bo = w.shape
    eye = jnp.eye(nb, dtype=w.dtype)
    return jnp.einsum("hij,hg->higj", w, eye).reshape(nb * bi, nb * bo)


def _pack_w_in(w_in):
    w_in = w_in.astype(BF16)
    d = w_in.shape[0]
    kiwi = w_in[:, SEG_KIWI:SEG_KIWI + IDX_DIM + IDX_HEADS]
    pad = jnp.zeros((d, LANES - IDX_DIM - IDX_HEADS), w_in.dtype)
    rest = w_in[:, SEG_KIWI + IDX_DIM + IDX_HEADS:]
    return jnp.concatenate([w_in[:, :SEG_KIWI], kiwi, pad, rest], axis=1)


def kernel(x, p, positions, g_pre_mix, w_in, conv_w, conv_b, w_rg_a, b_rg_a, w_rg_x, b_rg_x,
           lru_lambda, g_gmlp_v, w_spatial, b_spatial, w_branch, w_out, g_post_mix,
           g_pre_ffn, w_ffn_up, w_ffn_down, g_post_ffn, w_ple, w_ple_gate, g_post_ple):
    depth = w_in.shape[0]
    b, l, d = x.shape
    assert l % PROJ_TOKENS == 0 and l % DSA_KEYS == 0 and l % LRU_TOKENS == 0 and l % DSA_ROWS == 0
    assert l % POST_TOKENS == 0 and (b * l) % 1024 == 0 and DSA_KEYS >= INDEX_TOPK
    row = lambda a: a.reshape(1, -1)

    rope_c, rope_s = _rope_tables(positions)
    tril = jnp.tril(jnp.ones((CHUNK, CHUNK), dtype=bool))

    for i in range(depth):
        w_sp = jnp.where(tril[None], w_spatial[i], 0)
        w_sp = jnp.transpose(w_sp, (1, 0, 2)).reshape(CHUNK, GMLP_GROUPS * CHUNK).astype(BF16)
        b_sp = jnp.repeat(jnp.transpose(b_spatial[i]), GMLP_GROUP_DIM, axis=1)

        qt, k, vt, qit, ki, wit, xr, gr, yc, gates = _proj_call(
            x, row(g_pre_mix[i]), _pack_w_in(w_in[i]), rope_c, rope_s,
            row(g_gmlp_v[i]), w_sp, b_sp)

        ya = _dsa_call(qit, wit, qt, ki, k, vt)

        yb = _lru_call(xr, gr, conv_w[i], row(conv_b[i]),
                       _block_diag(w_rg_a[i]).astype(BF16), row(b_rg_a[i]),
                       _block_diag(w_rg_x[i]).astype(BF16), row(b_rg_x[i]),
                       row(lru_lambda[i]))

        consts = [w_branch[i].reshape(N_BRANCH * w_branch.shape[2], d).astype(BF16),
                  w_out[i].astype(BF16), row(g_post_mix[i]), row(g_pre_ffn[i]),
                  w_ffn_up[i].astype(BF16), w_ffn_down[i].astype(BF16), row(g_post_ffn[i]),
                  w_ple[i].astype(BF16), w_ple_gate[i].astype(BF16), row(g_post_ple[i])]
        x = _post_call(x, ya, yb, yc, gates, p, i, consts)
    return x
```

```python
import functools

import jax
import jax.numpy as jnp
import numpy as np
from jax import lax
from jax.experimental import pallas as pl
from jax.experimental.pallas import tpu as pltpu

F32 = jnp.float32
BF16 = jnp.bfloat16
I32 = jnp.int32
I16 = jnp.int16

N_HEADS = 8
HEAD_DIM = 64
ATTN_WIDTH = N_HEADS * HEAD_DIM
ROPE_DIM = HEAD_DIM // 4
ROPE_THETA = 500000.0
IDX_HEADS = 8
IDX_DIM = 64
INDEX_TOPK = 256
LRU_WIDTH = 512
LRU_BLOCKS = 8
CONV_WIDTH = 4
LRU_C = 8.0
GMLP_WIDTH = 512
GMLP_GROUPS = 8
GMLP_GROUP_DIM = GMLP_WIDTH // GMLP_GROUPS
CHUNK = 128
N_BRANCH = 3
EPS = 1e-6

LANES = 128
SUBLANES = 8
VMEM_LIMIT_BYTES = 56 * 1024 * 1024

INT_MIN = np.int32(-(2**31))
INT_MAX = np.int32(2**31 - 1)
MASK_BIAS = -1e30
Q_SCALE = float(np.log2(np.e)) * HEAD_DIM ** -0.5

PROJ_TOKENS = 512
DSA_ROWS = 256
DSA_KEYS = 512
DSA_SUB = 128
POST_TOKENS = 512
FFN_CHUNK = 1024

SEG_Q, SEG_K, SEG_V, SEG_QI = 0, 512, 1024, 1536
SEG_KIWI = 2048
SEG_XR, SEG_GR, SEG_ZU, SEG_ZV = 2176, 2688, 3200, 3712
SEG_GATE = 4224


def _rms(x, g):
    return x * lax.rsqrt(jnp.mean(x * x, axis=-1, keepdims=True) + EPS) * g


def _dot(a, b):
    return jnp.dot(a, b, preferred_element_type=F32)


def _params(n_grid):
    return pltpu.CompilerParams(
        dimension_semantics=("arbitrary",) * n_grid,
        vmem_limit_bytes=VMEM_LIMIT_BYTES)


def _const_spec(shape):
    zeros = (0,) * len(shape)
    return pl.BlockSpec(shape, lambda *_: zeros, pipeline_mode=pl.Buffered(1))


def _rope_table_kernel(pos_ref, invf_ref, c_ref, s_ref):
    ang = pos_ref[...].astype(F32) * invf_ref[...]
    lane = lax.broadcasted_iota(I32, ang.shape, 1) & (HEAD_DIM - 1)
    cos = jnp.cos(ang)
    sin = jnp.sin(ang)
    half = ROPE_DIM // 2
    c_ref[...] = jnp.where(lane < ROPE_DIM, cos, 1.0)
    s_ref[...] = jnp.where(lane < half, -sin, jnp.where(lane < ROPE_DIM, sin, 0.0))


def _rope_tables(positions):
    n = positions.size
    tile = 1024
    half = ROPE_DIM // 2
    inv_freq = ROPE_THETA ** (-jnp.arange(half, dtype=F32) * 2.0 / ROPE_DIM)
    inv_lane = jnp.tile(inv_freq, LANES // half).reshape(1, LANES)
    out = jax.ShapeDtypeStruct((n, LANES), F32)
    return pl.pallas_call(
        _rope_table_kernel,
        grid=(n // tile,),
        in_specs=[pl.BlockSpec((tile, 1), lambda i: (i, 0)),
                  pl.BlockSpec((1, LANES), lambda i: (0, 0))],
        out_specs=[pl.BlockSpec((tile, LANES), lambda i: (i, 0))] * 2,
        out_shape=[out, out],
        compiler_params=_params(1),
        name="rope_tables",
    )(positions.reshape(n, 1), inv_lane)


def _proj_kernel(x_ref, g_ref, w_ref, c_ref, s_ref, gv_ref, wsp_ref, bsp_ref,
                 cw_ref, cb_ref, wa_ref, ba_ref, wx_ref, bx_ref, lam_ref,
                 qt_ref, k_ref, vt_ref, qit_ref, ki_ref, wit_ref, yb_ref, yc_ref, gate_ref,
                 hist_ref, h_ref):
    t = x_ref.shape[0]
    h = _rms(x_ref[...], g_ref[...]).astype(BF16)
    cos = c_ref[...]
    sin = s_ref[...]
    lane = lax.broadcasted_iota(I32, (t, LANES), 1)
    first_half = (lane & (HEAD_DIM - 1)) < (ROPE_DIM // 2)

    def proj(start, width):
        return _dot(h, w_ref[:, start:start + width])

    def rope(p):
        cols = []
        for j in range(p.shape[1] // LANES):
            pj = p[:, j * LANES:(j + 1) * LANES]
            partner = jnp.where(first_half,
                                pltpu.roll(pj, LANES - ROPE_DIM // 2, 1),
                                pltpu.roll(pj, ROPE_DIM // 2, 1))
            cols.append(pj * cos + partner * sin)
        return cols[0] if len(cols) == 1 else jnp.concatenate(cols, axis=1)

    qt_ref[...] = (rope(proj(SEG_Q, ATTN_WIDTH)) * Q_SCALE).T.astype(BF16)
    k_ref[...] = rope(proj(SEG_K, ATTN_WIDTH)).astype(BF16)
    vt_ref[...] = proj(SEG_V, ATTN_WIDTH).T.astype(BF16)
    qit_ref[...] = rope(proj(SEG_QI, IDX_HEADS * IDX_DIM)).T.astype(BF16)

    kiwi = proj(SEG_KIWI, LANES)
    kiwi = jnp.where(lane < IDX_DIM, rope(kiwi), kiwi)
    ki_ref[...] = jnp.where(lane < IDX_DIM, kiwi, pltpu.roll(kiwi, IDX_DIM, 1)).astype(BF16)
    wit_ref[...] = kiwi.T[IDX_DIM:IDX_DIM + IDX_HEADS, :]

    _conv_rglru(proj(SEG_XR, LRU_WIDTH), proj(SEG_GR, LRU_WIDTH), cw_ref, cb_ref, wa_ref, ba_ref,
                wx_ref, bx_ref, lam_ref, yb_ref, hist_ref, h_ref)

    u = jax.nn.gelu(proj(SEG_ZU, GMLP_WIDTH))
    vn = _rms(jax.nn.gelu(proj(SEG_ZV, GMLP_WIDTH)), gv_ref[...]).astype(BF16)
    group = lax.broadcasted_iota(I32, (CHUNK, GMLP_WIDTH), 1) // GMLP_GROUP_DIM
    for c in range(t // CHUNK):
        vc = vn[c * CHUNK:(c + 1) * CHUNK, :]
        stack = jnp.concatenate(
            [jnp.where(group == g, vc, jnp.zeros_like(vc)) for g in range(GMLP_GROUPS)], axis=0)
        mixed = _dot(wsp_ref[...], stack) + bsp_ref[...]
        yc_ref[c * CHUNK:(c + 1) * CHUNK, :] = (u[c * CHUNK:(c + 1) * CHUNK, :] * mixed).astype(BF16)

    d = x_ref.shape[1]
    for n in range(N_BRANCH):
        gate_ref[:, n * d:(n + 1) * d] = jax.nn.sigmoid(proj(SEG_GATE + n * d, d)).astype(BF16)


def _proj_call(x, g_pre, w1, rope_c, rope_s, g_v, w_sp, b_sp, lru_consts):
    b, l, d = x.shape
    t = PROJ_TOKENS
    nt = l // t
    tok = lambda w: pl.BlockSpec((None, t, w), lambda bi, ti: (bi, ti, 0))
    tr = lambda w: pl.BlockSpec((None, w, t), lambda bi, ti: (bi, 0, ti))
    rope_spec = pl.BlockSpec((t, LANES), lambda bi, ti: (bi * nt + ti, 0))
    sds = lambda shape, dt: jax.ShapeDtypeStruct(shape, dt)
    return pl.pallas_call(
        _proj_kernel,
        grid=(b, nt),
        in_specs=[tok(d), _const_spec(g_pre.shape), _const_spec(w1.shape), rope_spec, rope_spec,
                  _const_spec(g_v.shape), _const_spec(w_sp.shape), _const_spec(b_sp.shape)]
                 + [_const_spec(c.shape) for c in lru_consts],
        out_specs=[tr(ATTN_WIDTH), tok(ATTN_WIDTH), tr(ATTN_WIDTH), tr(IDX_HEADS * IDX_DIM),
                   tok(LANES), tr(IDX_HEADS), tok(LRU_WIDTH), tok(GMLP_WIDTH),
                   tok(N_BRANCH * d)],
        out_shape=[sds((b, ATTN_WIDTH, l), BF16), sds((b, l, ATTN_WIDTH), BF16),
                   sds((b, ATTN_WIDTH, l), BF16), sds((b, IDX_HEADS * IDX_DIM, l), BF16),
                   sds((b, l, LANES), BF16), sds((b, IDX_HEADS, l), F32),
                   sds((b, l, LRU_WIDTH), BF16),
                   sds((b, l, GMLP_WIDTH), BF16), sds((b, l, N_BRANCH * d), BF16)],
        scratch_shapes=[pltpu.VMEM((SUBLANES, LRU_WIDTH), F32), pltpu.VMEM((SUBLANES, LRU_WIDTH), F32)],
        compiler_params=_params(2),
        name="in_proj",
    )(x, g_pre, w1, rope_c, rope_s, g_v, w_sp, b_sp, *lru_consts)


def _dsa_kernel(qit_ref, wit_ref, qt_ref, ki_ref, k_ref, vt_ref, o_ref,
                key_ref, hi_ref, lo_ref, s_ref, ot_ref, *, top_k):
    r = qt_ref.shape[1]
    l_keys = k_ref.shape[0]
    ck = DSA_KEYS
    sub = DSA_SUB
    nsub = ck // sub
    col0 = pl.program_id(1) * r
    nk = (col0 + r + ck - 1) // ck
    idx_scale = (IDX_DIM ** -0.5) * (IDX_HEADS ** -0.5)

    slab_row = lax.broadcasted_iota(I32, (LANES, r), 0)

    def head_weights(ref, h):
        slab = ref[(h // 2) * LANES:(h // 2 + 1) * LANES, :]
        keep = (slab_row < HEAD_DIM) if h % 2 == 0 else (slab_row >= HEAD_DIM)
        return jnp.where(keep, slab, jnp.zeros_like(slab))

    def chunk_loop(body, init):
        def wrapped(kc, carry):
            return body(pl.multiple_of(kc * ck, ck), carry)
        return lax.fori_loop(0, nk, wrapped, init)

    wit = wit_ref[...]
    w_idx = [head_weights(qit_ref, h) for h in range(IDX_HEADS)]
    qpos = col0 + lax.broadcasted_iota(I32, (sub, r), 1)
    krow = lax.broadcasted_iota(I32, (sub, r), 0)

    def score_chunk(k0, carry):
        for sb in range(nsub):
            k1 = k0 + sb * sub
            ki = ki_ref[pl.ds(k1, sub), :]
            acc = jnp.zeros((sub, r), F32)
            for h in range(IDX_HEADS):
                acc = acc + wit[h:h + 1, :] * jnp.maximum(_dot(ki, w_idx[h]), 0.0)
            bits = pltpu.bitcast(acc * idx_scale, I32)
            key = bits ^ ((bits >> 31) & INT_MAX)
            key = jnp.where(k1 + krow <= qpos, key, INT_MIN)
            key_ref[pl.ds(k1, sub), :] = key
            hi_ref[pl.ds(k1, sub), :] = (key >> 16).astype(I16)
            lo_ref[pl.ds(k1, sub), :] = ((key & 0xFFFF) - 32768).astype(I16)
        return carry

    chunk_loop(score_chunk, 0)

    n_acc = 4

    def count(ref, rows, pred):
        dt = ref.dtype
        one, zero = jnp.ones((), dt), jnp.zeros((), dt)

        def body(k0, accs):
            accs = list(accs)
            chunk = ref[pl.ds(k0, ck), :]
            for j in range(ck // rows):
                blk = chunk[j * rows:(j + 1) * rows, :]
                accs[j % n_acc] = accs[j % n_acc] + jnp.where(pred(blk, k0 + j * rows), one, zero)
            return tuple(accs)

        accs = chunk_loop(body, tuple(jnp.zeros((rows, r), dt) for _ in range(n_acc)))
        tot = accs[0].astype(I32)
        for a in accs[1:]:
            tot = tot + a.astype(I32)
        return jnp.sum(tot, axis=0, keepdims=True)

    def bisect16(ref, need):
        rows = 2 * SUBLANES

        def step(b, lo):
            cand = lo + lax.shift_left(jnp.int32(1), 15 - b)
            cand16 = jnp.broadcast_to(cand.astype(I16), (rows, r))
            n_ge = count(ref, rows, lambda blk, _: blk >= cand16)
            return jnp.where(n_ge >= need, cand, lo)

        return lax.fori_loop(0, 16, step, jnp.full((1, r), -32768, I32))

    thr_hi = bisect16(hi_ref, top_k)
    thr_hi16 = jnp.broadcast_to(thr_hi.astype(I16), (2 * SUBLANES, r))
    n_hi_gt = count(hi_ref, 2 * SUBLANES, lambda blk, _: blk > thr_hi16)

    def mask_low(k0, carry):
        sl = pl.ds(k0, ck)
        match = hi_ref[sl, :] == jnp.broadcast_to(thr_hi.astype(I16), (ck, r))
        lo_ref[sl, :] = jnp.where(match, lo_ref[sl, :], jnp.int16(-32768))
        return carry

    chunk_loop(mask_low, 0)
    thr_lo = bisect16(lo_ref, top_k - n_hi_gt)
    thr = thr_hi * 65536 + (thr_lo + 32768)

    thr8 = jnp.broadcast_to(thr, (SUBLANES, r))
    n_gt = count(key_ref, SUBLANES, lambda blk, _: blk > thr8)
    n_eq = count(key_ref, SUBLANES, lambda blk, _: blk == thr8)
    need = top_k - n_gt
    real = thr > INT_MIN
    has_tie = jnp.logical_and(real, n_eq > need)
    row8 = lax.broadcasted_iota(I32, (SUBLANES, r), 0)

    def tie_search():
        n_bits = max(1, int(l_keys - 1).bit_length())

        def pos_step(b, last):
            cand = last + lax.shift_left(jnp.int32(1), n_bits - 1 - b)
            n_before = count(key_ref, SUBLANES,
                             lambda blk, p0: jnp.logical_and(blk == thr8, p0 + row8 < cand))
            return jnp.where(n_before < need, cand, last)

        return lax.fori_loop(0, n_bits, pos_step, jnp.zeros((1, r), I32))

    any_tie = jnp.max(jnp.where(has_tie, 1.0, 0.0)) > 0.0
    key_row = lax.broadcasted_iota(I32, (ck, r), 0)

    def write_bias(bias_of):
        def bias_chunk(k0, carry):
            sl = pl.ds(k0, ck)
            key_ref[sl, :] = pltpu.bitcast(bias_of(key_ref[sl, :], k0), I32)
            return carry
        chunk_loop(bias_chunk, 0)

    def keep(selected):
        return jnp.where(selected, 0.0, MASK_BIAS).astype(F32)

    @pl.when(jnp.logical_not(any_tie))
    def _():
        thr_b = jnp.broadcast_to(jnp.where(real, thr, INT_MIN + 1), (ck, r))
        write_bias(lambda key, k0: keep(key >= thr_b))

    @pl.when(any_tie)
    def _():
        last = jnp.where(has_tie, tie_search(), jnp.where(real, INT_MAX, -1))
        thr_b = jnp.broadcast_to(thr, (ck, r))
        last_b = jnp.broadcast_to(last, (ck, r))
        write_bias(lambda key, k0: jnp.where(key == thr_b, keep(k0 + key_row <= last_b),
                                             keep(key > thr_b)))

    w_q = [head_weights(qt_ref, h) for h in range(N_HEADS)]
    pack = 2 * SUBLANES

    def fold8(a):
        out = a[0:SUBLANES, :]
        for j in range(1, a.shape[0] // SUBLANES):
            out = jnp.maximum(out, a[j * SUBLANES:(j + 1) * SUBLANES, :])
        return out

    def logits_chunk(k0, ms):
        ms = list(ms)
        for sb in range(nsub):
            k1 = k0 + sb * sub
            bias = pltpu.bitcast(key_ref[pl.ds(k1, sub), :], F32)
            for h in range(N_HEADS):
                pair = (h // 2) * LANES
                s = _dot(k_ref[pl.ds(k1, sub), pair:pair + LANES], w_q[h]) + bias
                s_ref[h, pl.ds(k1, sub), :] = s.astype(BF16)
                ms[h] = jnp.maximum(ms[h], fold8(s))
        return tuple(ms)

    ms = chunk_loop(logits_chunk,
                    tuple(jnp.full((SUBLANES, r), -jnp.inf, F32) for _ in range(N_HEADS)))
    ms = [jnp.broadcast_to(jnp.max(m, axis=0, keepdims=True).astype(BF16), (ck, r)) for m in ms]

    ot_ref[...] = jnp.zeros_like(ot_ref)
    ones_rows = jnp.ones((pack, ck), BF16)

    def pv_chunk(k0, carry):
        sl = pl.ds(k0, ck)
        for h in range(N_HEADS):
            p = jnp.exp2(s_ref[h, sl, :] - ms[h])
            lhs = jnp.concatenate([vt_ref[h * HEAD_DIM:(h + 1) * HEAD_DIM, sl], ones_rows], axis=0)
            ot_ref[h] += _dot(lhs, p)
        return carry

    chunk_loop(pv_chunk, 0)
    outs = []
    for h in range(N_HEADS):
        acc = ot_ref[h]
        outs.append(acc[0:HEAD_DIM, :] * (1.0 / acc[HEAD_DIM:HEAD_DIM + 1, :]))
    o_ref[...] = jnp.concatenate(outs, axis=0).T.astype(BF16)


def _dsa_call(qit, wit, qt, ki, k, vt):
    b, l, _ = k.shape
    r = DSA_ROWS
    top_k = min(INDEX_TOPK, l // 4)
    col = lambda h: pl.BlockSpec((None, h, r), lambda bi, ti: (bi, 0, ti))
    full = lambda h, w: pl.BlockSpec((None, h, w), lambda bi, ti: (bi, 0, 0),
                                     pipeline_mode=pl.Buffered(1))
    return pl.pallas_call(
        functools.partial(_dsa_kernel, top_k=top_k),
        grid=(b, l // r),
        in_specs=[col(IDX_HEADS * IDX_DIM), col(IDX_HEADS), col(ATTN_WIDTH),
                  full(l, LANES), full(l, ATTN_WIDTH), full(ATTN_WIDTH, l)],
        out_specs=pl.BlockSpec((None, r, ATTN_WIDTH), lambda bi, ti: (bi, ti, 0)),
        out_shape=jax.ShapeDtypeStruct((b, l, ATTN_WIDTH), BF16),
        scratch_shapes=[pltpu.VMEM((l, r), I32), pltpu.VMEM((l, r), I16), pltpu.VMEM((l, r), I16),
                        pltpu.VMEM((N_HEADS, l, r), BF16),
                        pltpu.VMEM((N_HEADS, HEAD_DIM + 2 * SUBLANES, r), F32)],
        compiler_params=_params(2),
        name="dsa_attention",
    )(qit, wit, qt, ki, k, vt)


def _conv_rglru(x, gr, cw_ref, cb_ref, wa_ref, ba_ref, wx_ref, bx_ref, lam_ref,
                y_ref, hist_ref, h_ref):
    t = x.shape[0]

    @pl.when(pl.program_id(1) == 0)
    def _():
        hist_ref[...] = jnp.zeros_like(hist_ref)
        h_ref[...] = jnp.zeros_like(h_ref)

    hist = hist_ref[...]
    row = lax.broadcasted_iota(I32, x.shape, 0)
    row8 = lax.broadcasted_iota(I32, hist.shape, 0)
    cw = cw_ref[...]
    conv = x * cw[CONV_WIDTH - 1:CONV_WIDTH, :] + cb_ref[...]
    for d in range(1, CONV_WIDTH):
        shifted = pltpu.roll(x, d, 0)
        head = jnp.where(row8 < d, pltpu.roll(hist, d, 0), shifted[0:SUBLANES, :])
        shifted = jnp.concatenate([head, shifted[SUBLANES:, :]], axis=0)
        conv = conv + shifted * cw[CONV_WIDTH - 1 - d:CONV_WIDTH - d, :]
    hist_ref[...] = x[t - SUBLANES:, :]

    cb16 = conv.astype(BF16)
    rg = jax.nn.sigmoid(_dot(cb16, wa_ref[...]) + ba_ref[...])
    ig = jax.nn.sigmoid(_dot(cb16, wx_ref[...]) + bx_ref[...])
    log_a = -LRU_C * rg * jax.nn.softplus(-lam_ref[...])
    a = jnp.exp(log_a)
    inp = jnp.sqrt(-jnp.tanh(log_a) * (a * a + 1.0)) * (ig * conv)

    in_group = row & (SUBLANES - 1)
    d = 1
    while d < SUBLANES:
        ok = in_group >= d
        inp = jnp.where(ok, a * pltpu.roll(inp, d, 0) + inp, inp)
        a = jnp.where(ok, a * pltpu.roll(a, d, 0), a)
        d *= 2
    carry = h_ref[0:1, :]
    gelu_g = jax.nn.gelu(gr)
    pack = 2 * SUBLANES
    for g in range(t // pack):
        halves = []
        for half in range(2):
            rows = slice(g * pack + half * SUBLANES, g * pack + (half + 1) * SUBLANES)
            hg = inp[rows, :] + a[rows, :] * carry
            carry = hg[SUBLANES - 1:SUBLANES, :]
            halves.append(hg * gelu_g[rows, :])
        y_ref[g * pack:(g + 1) * pack, :] = jnp.concatenate(halves, axis=0).astype(BF16)
    h_ref[...] = jnp.broadcast_to(carry, h_ref.shape)


def _post_kernel(x_ref, ya_ref, yb_ref, yc_ref, gate_ref, p_ref,
                 wb_ref, wo_ref, gpm_ref, gpf_ref, wup_ref, wdn_ref, gpo_ref,
                 wple_ref, wpg_ref, gpp_ref, o_ref):
    d = x_ref.shape[1]
    w = ya_ref.shape[1]
    x = x_ref[...]
    merged = jnp.zeros(x.shape, F32)
    for n, y_ref in enumerate((ya_ref, yb_ref, yc_ref)):
        branch = _dot(y_ref[...], wb_ref[n * w:(n + 1) * w, :])
        merged = merged + gate_ref[:, n * d:(n + 1) * d].astype(F32) * branch
    x = x + _rms(_dot(merged.astype(BF16), wo_ref[...]), gpm_ref[...])

    h2 = _rms(x, gpf_ref[...]).astype(BF16)
    f = jnp.zeros(x.shape, F32)
    for c in range(wup_ref.shape[1] // FFN_CHUNK):
        sl = slice(c * FFN_CHUNK, (c + 1) * FFN_CHUNK)
        hid = jnp.square(jnp.maximum(_dot(h2, wup_ref[:, sl]), 0.0)).astype(BF16)
        f = f + _dot(hid, wdn_ref[sl, :])
    x = x + _rms(f, gpo_ref[...])

    ple = _dot(p_ref[...].astype(BF16), wple_ref[...]) * jax.nn.sigmoid(
        _dot(x.astype(BF16), wpg_ref[...]))
    o_ref[...] = x + _rms(ple, gpp_ref[...])


def _post_call(x, ya, yb, yc, gates, p, layer, consts):
    b, l, d = x.shape
    t = POST_TOKENS
    tok = lambda w: pl.BlockSpec((None, t, w), lambda bi, ti: (bi, ti, 0))
    p_spec = pl.BlockSpec((None, None, t, p.shape[3]), lambda bi, ti: (layer, bi, ti, 0))
    return pl.pallas_call(
        _post_kernel,
        grid=(b, l // t),
        in_specs=[tok(d), tok(ya.shape[2]), tok(yb.shape[2]), tok(yc.shape[2]),
                  tok(gates.shape[2]), p_spec] + [_const_spec(c.shape) for c in consts],
        out_specs=tok(d),
        out_shape=jax.ShapeDtypeStruct((b, l, d), F32),
        compiler_params=_params(2),
        name="merge_ffn_ple",
    )(x, ya, yb, yc, gates, p, *consts)


def _block_diag(w):
    nb, bi, bo = w.shape
    eye = jnp.eye(nb, dtype=w.dtype)
    return jnp.einsum("hij,hg->higj", w, eye).reshape(nb * bi, nb * bo)


def _pack_w_in(w_in):
    w_in = w_in.astype(BF16)
    d = w_in.shape[0]
    kiwi = w_in[:, SEG_KIWI:SEG_KIWI + IDX_DIM + IDX_HEADS]
    pad = jnp.zeros((d, LANES - IDX_DIM - IDX_HEADS), w_in.dtype)
    rest = w_in[:, SEG_KIWI + IDX_DIM + IDX_HEADS:]
    return jnp.concatenate([w_in[:, :SEG_KIWI], kiwi, pad, rest], axis=1)


def kernel(x, p, positions, g_pre_mix, w_in, conv_w, conv_b, w_rg_a, b_rg_a, w_rg_x, b_rg_x,
           lru_lambda, g_gmlp_v, w_spatial, b_spatial, w_branch, w_out, g_post_mix,
           g_pre_ffn, w_ffn_up, w_ffn_down, g_post_ffn, w_ple, w_ple_gate, g_post_ple):
    depth = w_in.shape[0]
    b, l, d = x.shape
    assert l % PROJ_TOKENS == 0 and l % DSA_KEYS == 0 and l % DSA_ROWS == 0
    assert l % POST_TOKENS == 0 and (b * l) % 1024 == 0 and DSA_KEYS >= INDEX_TOPK
    row = lambda a: a.reshape(1, -1)

    rope_c, rope_s = _rope_tables(positions)
    tril = jnp.tril(jnp.ones((CHUNK, CHUNK), dtype=bool))

    for i in range(depth):
        w_sp = jnp.where(tril[None], w_spatial[i], 0)
        w_sp = jnp.transpose(w_sp, (1, 0, 2)).reshape(CHUNK, GMLP_GROUPS * CHUNK).astype(BF16)
        b_sp = jnp.repeat(jnp.transpose(b_spatial[i]), GMLP_GROUP_DIM, axis=1)

        lru_consts = [conv_w[i], row(conv_b[i]),
                      _block_diag(w_rg_a[i]).astype(BF16), row(b_rg_a[i]),
                      _block_diag(w_rg_x[i]).astype(BF16), row(b_rg_x[i]), row(lru_lambda[i])]
        qt, k, vt, qit, ki, wit, yb, yc, gates = _proj_call(
            x, row(g_pre_mix[i]), _pack_w_in(w_in[i]), rope_c, rope_s,
            row(g_gmlp_v[i]), w_sp, b_sp, lru_consts)

        ya = _dsa_call(qit, wit, qt, ki, k, vt)

        consts = [w_branch[i].reshape(N_BRANCH * w_branch.shape[2], d).astype(BF16),
                  w_out[i].astype(BF16), row(g_post_mix[i]), row(g_pre_ffn[i]),
                  w_ffn_up[i].astype(BF16), w_ffn_down[i].astype(BF16), row(g_post_ffn[i]),
                  w_ple[i].astype(BF16), w_ple_gate[i].astype(BF16), row(g_post_ple[i])]
        x = _post_call(x, ya, yb, yc, gates, p, i, consts)
    return x
```

```python
import functools

import jax
import jax.numpy as jnp
import numpy as np
from jax import lax
from jax.experimental import pallas as pl
from jax.experimental.pallas import tpu as pltpu

F32 = jnp.float32
BF16 = jnp.bfloat16
I32 = jnp.int32
I16 = jnp.int16

N_HEADS = 8
HEAD_DIM = 64
ATTN_WIDTH = N_HEADS * HEAD_DIM
ROPE_DIM = HEAD_DIM // 4
ROPE_THETA = 500000.0
IDX_HEADS = 8
IDX_DIM = 64
INDEX_TOPK = 256
LRU_WIDTH = 512
LRU_BLOCKS = 8
CONV_WIDTH = 4
LRU_C = 8.0
GMLP_WIDTH = 512
GMLP_GROUPS = 8
GMLP_GROUP_DIM = GMLP_WIDTH // GMLP_GROUPS
CHUNK = 128
N_BRANCH = 3
EPS = 1e-6

LANES = 128
SUBLANES = 8
VMEM_LIMIT_BYTES = 56 * 1024 * 1024

INT_MIN = np.int32(-(2**31))
INT_MAX = np.int32(2**31 - 1)
MASK_BIAS = -1e30
Q_SCALE = float(np.log2(np.e)) * HEAD_DIM ** -0.5

PROJ_TOKENS = 256
DSA_ROWS = 256
DSA_KEYS = 512
DSA_SUB = 128
POST_TOKENS = 512
FFN_CHUNK = 1024

SEG_Q, SEG_K, SEG_V, SEG_QI = 0, 512, 1024, 1536
SEG_KIWI = 2048
SEG_XR, SEG_GR, SEG_ZU, SEG_ZV = 2176, 2688, 3200, 3712
SEG_GATE = 4224


def _rms(x, g):
    return x * lax.rsqrt(jnp.mean(x * x, axis=-1, keepdims=True) + EPS) * g


def _dot(a, b):
    return jnp.dot(a, b, preferred_element_type=F32)


def _params(n_grid):
    return pltpu.CompilerParams(
        dimension_semantics=("arbitrary",) * n_grid,
        vmem_limit_bytes=VMEM_LIMIT_BYTES)


def _const_spec(shape):
    zeros = (0,) * len(shape)
    return pl.BlockSpec(shape, lambda *_: zeros, pipeline_mode=pl.Buffered(1))


def _rope_table_kernel(pos_ref, invf_ref, c_ref, s_ref):
    ang = pos_ref[...].astype(F32) * invf_ref[...]
    lane = lax.broadcasted_iota(I32, ang.shape, 1) & (HEAD_DIM - 1)
    cos = jnp.cos(ang)
    sin = jnp.sin(ang)
    half = ROPE_DIM // 2
    c_ref[...] = jnp.where(lane < ROPE_DIM, cos, 1.0)
    s_ref[...] = jnp.where(lane < half, -sin, jnp.where(lane < ROPE_DIM, sin, 0.0))


def _rope_tables(positions):
    n = positions.size
    tile = 1024
    half = ROPE_DIM // 2
    inv_freq = ROPE_THETA ** (-jnp.arange(half, dtype=F32) * 2.0 / ROPE_DIM)
    inv_lane = jnp.tile(inv_freq, LANES // half).reshape(1, LANES)
    out = jax.ShapeDtypeStruct((n, LANES), F32)
    return pl.pallas_call(
        _rope_table_kernel,
        grid=(n // tile,),
        in_specs=[pl.BlockSpec((tile, 1), lambda i: (i, 0)),
                  pl.BlockSpec((1, LANES), lambda i: (0, 0))],
        out_specs=[pl.BlockSpec((tile, LANES), lambda i: (i, 0))] * 2,
        out_shape=[out, out],
        compiler_params=_params(1),
        name="rope_tables",
    )(positions.reshape(n, 1), inv_lane)


def _proj_kernel(x_ref, g_ref, w_ref, c_ref, s_ref, gv_ref, wsp_ref, bsp_ref,
                 cw_ref, cb_ref, wa_ref, ba_ref, wx_ref, bx_ref, lam_ref,
                 qt_ref, k_ref, vt_ref, qit_ref, ki_ref, wit_ref, yb_ref, yc_ref, gate_ref,
                 hist_ref, h_ref):
    t = x_ref.shape[0]
    h = _rms(x_ref[...], g_ref[...]).astype(BF16)
    cos = c_ref[...]
    sin = s_ref[...]
    lane = lax.broadcasted_iota(I32, (t, LANES), 1)
    first_half = (lane & (HEAD_DIM - 1)) < (ROPE_DIM // 2)

    def proj(start, width):
        return _dot(h, w_ref[:, start:start + width])

    def rope(p):
        cols = []
        for j in range(p.shape[1] // LANES):
            pj = p[:, j * LANES:(j + 1) * LANES]
            partner = jnp.where(first_half,
                                pltpu.roll(pj, LANES - ROPE_DIM // 2, 1),
                                pltpu.roll(pj, ROPE_DIM // 2, 1))
            cols.append(pj * cos + partner * sin)
        return cols[0] if len(cols) == 1 else jnp.concatenate(cols, axis=1)

    qt_ref[...] = (rope(proj(SEG_Q, ATTN_WIDTH)) * Q_SCALE).T.astype(BF16)
    k_ref[...] = rope(proj(SEG_K, ATTN_WIDTH)).astype(BF16)
    vt_ref[...] = proj(SEG_V, ATTN_WIDTH).T.astype(BF16)
    qit_ref[...] = rope(proj(SEG_QI, IDX_HEADS * IDX_DIM)).T.astype(BF16)

    kiwi = proj(SEG_KIWI, LANES)
    kiwi = jnp.where(lane < IDX_DIM, rope(kiwi), kiwi)
    ki_ref[...] = jnp.where(lane < IDX_DIM, kiwi, pltpu.roll(kiwi, IDX_DIM, 1)).astype(BF16)
    wit_ref[...] = kiwi.T[IDX_DIM:IDX_DIM + IDX_HEADS, :]

    _conv_rglru(proj(SEG_XR, LRU_WIDTH), proj(SEG_GR, LRU_WIDTH), cw_ref, cb_ref, wa_ref, ba_ref,
                wx_ref, bx_ref, lam_ref, yb_ref, hist_ref, h_ref)

    u = jax.nn.gelu(proj(SEG_ZU, GMLP_WIDTH))
    vn = _rms(jax.nn.gelu(proj(SEG_ZV, GMLP_WIDTH)), gv_ref[...]).astype(BF16)
    group = lax.broadcasted_iota(I32, (CHUNK, GMLP_WIDTH), 1) // GMLP_GROUP_DIM
    for c in range(t // CHUNK):
        vc = vn[c * CHUNK:(c + 1) * CHUNK, :]
        stack = jnp.concatenate(
            [jnp.where(group == g, vc, jnp.zeros_like(vc)) for g in range(GMLP_GROUPS)], axis=0)
        mixed = _dot(wsp_ref[...], stack) + bsp_ref[...]
        yc_ref[c * CHUNK:(c + 1) * CHUNK, :] = (u[c * CHUNK:(c + 1) * CHUNK, :] * mixed).astype(BF16)

    d = x_ref.shape[1]
    for n in range(N_BRANCH):
        gate_ref[:, n * d:(n + 1) * d] = jax.nn.sigmoid(proj(SEG_GATE + n * d, d)).astype(BF16)


def _proj_call(x, g_pre, w1, rope_c, rope_s, g_v, w_sp, b_sp, lru_consts):
    b, l, d = x.shape
    t = PROJ_TOKENS
    nt = l // t
    tok = lambda w: pl.BlockSpec((None, t, w), lambda bi, ti: (bi, ti, 0))
    tr = lambda w: pl.BlockSpec((None, w, t), lambda bi, ti: (bi, 0, ti))
    rope_spec = pl.BlockSpec((t, LANES), lambda bi, ti: (bi * nt + ti, 0))
    sds = lambda shape, dt: jax.ShapeDtypeStruct(shape, dt)
    return pl.pallas_call(
        _proj_kernel,
        grid=(b, nt),
        in_specs=[tok(d), _const_spec(g_pre.shape), _const_spec(w1.shape), rope_spec, rope_spec,
                  _const_spec(g_v.shape), _const_spec(w_sp.shape), _const_spec(b_sp.shape)]
                 + [_const_spec(c.shape) for c in lru_consts],
        out_specs=[tr(ATTN_WIDTH), tok(ATTN_WIDTH), tr(ATTN_WIDTH), tr(IDX_HEADS * IDX_DIM),
                   tok(LANES), tr(IDX_HEADS), tok(LRU_WIDTH), tok(GMLP_WIDTH),
                   tok(N_BRANCH * d)],
        out_shape=[sds((b, ATTN_WIDTH, l), BF16), sds((b, l, ATTN_WIDTH), BF16),
                   sds((b, ATTN_WIDTH, l), BF16), sds((b, IDX_HEADS * IDX_DIM, l), BF16),
                   sds((b, l, LANES), BF16), sds((b, IDX_HEADS, l), F32),
                   sds((b, l, LRU_WIDTH), BF16),
                   sds((b, l, GMLP_WIDTH), BF16), sds((b, l, N_BRANCH * d), BF16)],
        scratch_shapes=[pltpu.VMEM((SUBLANES, LRU_WIDTH), F32), pltpu.VMEM((SUBLANES, LRU_WIDTH), F32)],
        compiler_params=_params(2),
        name="in_proj",
    )(x, g_pre, w1, rope_c, rope_s, g_v, w_sp, b_sp, *lru_consts)


def _dsa_kernel(qit_ref, wit_ref, qt_ref, ki_ref, k_ref, vt_ref, o_ref,
                key_ref, hi_ref, lo_ref, s_ref, ot_ref, *, top_k):
    r = qt_ref.shape[1]
    l_keys = k_ref.shape[0]
    ck = DSA_KEYS
    sub = DSA_SUB
    nsub = ck // sub
    col0 = pl.program_id(1) * r
    nk = (col0 + r + ck - 1) // ck
    idx_scale = (IDX_DIM ** -0.5) * (IDX_HEADS ** -0.5)

    slab_row = lax.broadcasted_iota(I32, (LANES, r), 0)

    def head_weights(ref, h):
        slab = ref[(h // 2) * LANES:(h // 2 + 1) * LANES, :]
        keep = (slab_row < HEAD_DIM) if h % 2 == 0 else (slab_row >= HEAD_DIM)
        return jnp.where(keep, slab, jnp.zeros_like(slab))

    def chunk_loop(body, init):
        def wrapped(kc, carry):
            return body(pl.multiple_of(kc * ck, ck), carry)
        return lax.fori_loop(0, nk, wrapped, init)

    wit = wit_ref[...]
    w_idx = [head_weights(qit_ref, h) for h in range(IDX_HEADS)]
    qpos = col0 + lax.broadcasted_iota(I32, (sub, r), 1)
    krow = lax.broadcasted_iota(I32, (sub, r), 0)

    def score_chunk(k0, carry):
        for sb in range(nsub):
            k1 = k0 + sb * sub
            ki = ki_ref[pl.ds(k1, sub), :]
            acc = jnp.zeros((sub, r), F32)
            for h in range(IDX_HEADS):
                acc = acc + wit[h:h + 1, :] * jnp.maximum(_dot(ki, w_idx[h]), 0.0)
            bits = pltpu.bitcast(acc * idx_scale, I32)
            key = bits ^ ((bits >> 31) & INT_MAX)
            key = jnp.where(k1 + krow <= qpos, key, INT_MIN)
            key_ref[pl.ds(k1, sub), :] = key
            hi_ref[pl.ds(k1, sub), :] = (key >> 16).astype(I16)
            lo_ref[pl.ds(k1, sub), :] = ((key & 0xFFFF) - 32768).astype(I16)
        return carry

    chunk_loop(score_chunk, 0)

    n_acc = 4

    def count(ref, rows, pred):
        dt = ref.dtype
        one, zero = jnp.ones((), dt), jnp.zeros((), dt)

        def body(k0, accs):
            accs = list(accs)
            chunk = ref[pl.ds(k0, ck), :]
            for j in range(ck // rows):
                blk = chunk[j * rows:(j + 1) * rows, :]
                accs[j % n_acc] = accs[j % n_acc] + jnp.where(pred(blk, k0 + j * rows), one, zero)
            return tuple(accs)

        accs = chunk_loop(body, tuple(jnp.zeros((rows, r), dt) for _ in range(n_acc)))
        tot = accs[0].astype(I32)
        for a in accs[1:]:
            tot = tot + a.astype(I32)
        return jnp.sum(tot, axis=0, keepdims=True)

    def bisect16(ref, need):
        rows = 2 * SUBLANES

        def step(b, lo):
            cand = lo + lax.shift_left(jnp.int32(1), 15 - b)
            cand16 = jnp.broadcast_to(cand.astype(I16), (rows, r))
            n_ge = count(ref, rows, lambda blk, _: blk >= cand16)
            return jnp.where(n_ge >= need, cand, lo)

        return lax.fori_loop(0, 16, step, jnp.full((1, r), -32768, I32))

    thr_hi = bisect16(hi_ref, top_k)
    thr_hi16 = jnp.broadcast_to(thr_hi.astype(I16), (2 * SUBLANES, r))
    n_hi_gt = count(hi_ref, 2 * SUBLANES, lambda blk, _: blk > thr_hi16)

    def mask_low(k0, carry):
        sl = pl.ds(k0, ck)
        match = hi_ref[sl, :] == jnp.broadcast_to(thr_hi.astype(I16), (ck, r))
        lo_ref[sl, :] = jnp.where(match, lo_ref[sl, :], jnp.int16(-32768))
        return carry

    chunk_loop(mask_low, 0)
    thr_lo = bisect16(lo_ref, top_k - n_hi_gt)
    thr = thr_hi * 65536 + (thr_lo + 32768)

    thr8 = jnp.broadcast_to(thr, (SUBLANES, r))
    n_gt = count(key_ref, SUBLANES, lambda blk, _: blk > thr8)
    n_eq = count(key_ref, SUBLANES, lambda blk, _: blk == thr8)
    need = top_k - n_gt
    real = thr > INT_MIN
    has_tie = jnp.logical_and(real, n_eq > need)
    row8 = lax.broadcasted_iota(I32, (SUBLANES, r), 0)

    def tie_search():
        n_bits = max(1, int(l_keys - 1).bit_length())

        def pos_step(b, last):
            cand = last + lax.shift_left(jnp.int32(1), n_bits - 1 - b)
            n_before = count(key_ref, SUBLANES,
                             lambda blk, p0: jnp.logical_and(blk == thr8, p0 + row8 < cand))
            return jnp.where(n_before < need, cand, last)

        return lax.fori_loop(0, n_bits, pos_step, jnp.zeros((1, r), I32))

    any_tie = jnp.max(jnp.where(has_tie, 1.0, 0.0)) > 0.0
    key_row = lax.broadcasted_iota(I32, (ck, r), 0)

    def write_bias(bias_of):
        def bias_chunk(k0, carry):
            sl = pl.ds(k0, ck)
            key_ref[sl, :] = pltpu.bitcast(bias_of(key_ref[sl, :], k0), I32)
            return carry
        chunk_loop(bias_chunk, 0)

    def keep(selected):
        return jnp.where(selected, 0.0, MASK_BIAS).astype(F32)

    @pl.when(jnp.logical_not(any_tie))
    def _():
        thr_b = jnp.broadcast_to(jnp.where(real, thr, INT_MIN + 1), (ck, r))
        write_bias(lambda key, k0: keep(key >= thr_b))

    @pl.when(any_tie)
    def _():
        last = jnp.where(has_tie, tie_search(), jnp.where(real, INT_MAX, -1))
        thr_b = jnp.broadcast_to(thr, (ck, r))
        last_b = jnp.broadcast_to(last, (ck, r))
        write_bias(lambda key, k0: jnp.where(key == thr_b, keep(k0 + key_row <= last_b),
                                             keep(key > thr_b)))

    w_q = [head_weights(qt_ref, h) for h in range(N_HEADS)]
    pack = 2 * SUBLANES

    def fold8(a):
        out = a[0:SUBLANES, :]
        for j in range(1, a.shape[0] // SUBLANES):
            out = jnp.maximum(out, a[j * SUBLANES:(j + 1) * SUBLANES, :])
        return out

    def logits_chunk(k0, ms):
        ms = list(ms)
        for sb in range(nsub):
            k1 = k0 + sb * sub
            bias = pltpu.bitcast(key_ref[pl.ds(k1, sub), :], F32)
            for h in range(N_HEADS):
                pair = (h // 2) * LANES
                s = _dot(k_ref[pl.ds(k1, sub), pair:pair + LANES], w_q[h]) + bias
                s_ref[h, pl.ds(k1, sub), :] = s.astype(BF16)
                ms[h] = jnp.maximum(ms[h], fold8(s))
        return tuple(ms)

    ms = chunk_loop(logits_chunk,
                    tuple(jnp.full((SUBLANES, r), -jnp.inf, F32) for _ in range(N_HEADS)))
    ms = [jnp.broadcast_to(jnp.max(m, axis=0, keepdims=True).astype(BF16), (ck, r)) for m in ms]

    ot_ref[...] = jnp.zeros_like(ot_ref)
    ones_rows = jnp.ones((pack, ck), BF16)

    def pv_chunk(k0, carry):
        sl = pl.ds(k0, ck)
        for h in range(N_HEADS):
            p = jnp.exp2(s_ref[h, sl, :] - ms[h])
            lhs = jnp.concatenate([vt_ref[h * HEAD_DIM:(h + 1) * HEAD_DIM, sl], ones_rows], axis=0)
            ot_ref[h] += _dot(lhs, p)
        return carry

    chunk_loop(pv_chunk, 0)
    outs = []
    for h in range(N_HEADS):
        acc = ot_ref[h]
        outs.append(acc[0:HEAD_DIM, :] * (1.0 / acc[HEAD_DIM:HEAD_DIM + 1, :]))
    o_ref[...] = jnp.concatenate(outs, axis=0).T.astype(BF16)


def _dsa_call(qit, wit, qt, ki, k, vt):
    b, l, _ = k.shape
    r = DSA_ROWS
    top_k = min(INDEX_TOPK, l // 4)
    col = lambda h: pl.BlockSpec((None, h, r), lambda bi, ti: (bi, 0, ti))
    full = lambda h, w: pl.BlockSpec((None, h, w), lambda bi, ti: (bi, 0, 0),
                                     pipeline_mode=pl.Buffered(1))
    return pl.pallas_call(
        functools.partial(_dsa_kernel, top_k=top_k),
        grid=(b, l // r),
        in_specs=[col(IDX_HEADS * IDX_DIM), col(IDX_HEADS), col(ATTN_WIDTH),
                  full(l, LANES), full(l, ATTN_WIDTH), full(ATTN_WIDTH, l)],
        out_specs=pl.BlockSpec((None, r, ATTN_WIDTH), lambda bi, ti: (bi, ti, 0)),
        out_shape=jax.ShapeDtypeStruct((b, l, ATTN_WIDTH), BF16),
        scratch_shapes=[pltpu.VMEM((l, r), I32), pltpu.VMEM((l, r), I16), pltpu.VMEM((l, r), I16),
                        pltpu.VMEM((N_HEADS, l, r), BF16),
                        pltpu.VMEM((N_HEADS, HEAD_DIM + 2 * SUBLANES, r), F32)],
        compiler_params=_params(2),
        name="dsa_attention",
    )(qit, wit, qt, ki, k, vt)


def _conv_rglru(x, gr, cw_ref, cb_ref, wa_ref, ba_ref, wx_ref, bx_ref, lam_ref,
                y_ref, hist_ref, h_ref):
    t = x.shape[0]

    @pl.when(pl.program_id(1) == 0)
    def _():
        hist_ref[...] = jnp.zeros_like(hist_ref)
        h_ref[...] = jnp.zeros_like(h_ref)

    hist = hist_ref[...]
    row = lax.broadcasted_iota(I32, x.shape, 0)
    row8 = lax.broadcasted_iota(I32, hist.shape, 0)
    cw = cw_ref[...]
    conv = x * cw[CONV_WIDTH - 1:CONV_WIDTH, :] + cb_ref[...]
    for d in range(1, CONV_WIDTH):
        shifted = pltpu.roll(x, d, 0)
        head = jnp.where(row8 < d, pltpu.roll(hist, d, 0), shifted[0:SUBLANES, :])
        shifted = jnp.concatenate([head, shifted[SUBLANES:, :]], axis=0)
        conv = conv + shifted * cw[CONV_WIDTH - 1 - d:CONV_WIDTH - d, :]
    hist_ref[...] = x[t - SUBLANES:, :]

    cb16 = conv.astype(BF16)
    rg = jax.nn.sigmoid(_dot(cb16, wa_ref[...]) + ba_ref[...])
    ig = jax.nn.sigmoid(_dot(cb16, wx_ref[...]) + bx_ref[...])
    log_a = -LRU_C * rg * jax.nn.softplus(-lam_ref[...])
    a = jnp.exp(log_a)
    inp = jnp.sqrt(-jnp.tanh(log_a) * (a * a + 1.0)) * (ig * conv)

    in_group = row & (SUBLANES - 1)
    d = 1
    while d < SUBLANES:
        ok = in_group >= d
        inp = jnp.where(ok, a * pltpu.roll(inp, d, 0) + inp, inp)
        a = jnp.where(ok, a * pltpu.roll(a, d, 0), a)
        d *= 2
    carry = h_ref[0:1, :]
    gelu_g = jax.nn.gelu(gr)
    pack = 2 * SUBLANES
    for g in range(t // pack):
        halves = []
        for half in range(2):
            rows = slice(g * pack + half * SUBLANES, g * pack + (half + 1) * SUBLANES)
            hg = inp[rows, :] + a[rows, :] * carry
            carry = hg[SUBLANES - 1:SUBLANES, :]
            halves.append(hg * gelu_g[rows, :])
        y_ref[g * pack:(g + 1) * pack, :] = jnp.concatenate(halves, axis=0).astype(BF16)
    h_ref[...] = jnp.broadcast_to(carry, h_ref.shape)


def _post_kernel(x_ref, ya_ref, yb_ref, yc_ref, gate_ref, p_ref,
                 wb_ref, wo_ref, gpm_ref, gpf_ref, wup_ref, wdn_ref, gpo_ref,
                 wple_ref, wpg_ref, gpp_ref, o_ref):
    d = x_ref.shape[1]
    w = ya_ref.shape[1]
    x = x_ref[...]
    merged = jnp.zeros(x.shape, F32)
    for n, y_ref in enumerate((ya_ref, yb_ref, yc_ref)):
        branch = _dot(y_ref[...], wb_ref[n * w:(n + 1) * w, :])
        merged = merged + gate_ref[:, n * d:(n + 1) * d].astype(F32) * branch
    x = x + _rms(_dot(merged.astype(BF16), wo_ref[...]), gpm_ref[...])

    h2 = _rms(x, gpf_ref[...]).astype(BF16)
    f = jnp.zeros(x.shape, F32)
    for c in range(wup_ref.shape[1] // FFN_CHUNK):
        sl = slice(c * FFN_CHUNK, (c + 1) * FFN_CHUNK)
        hid = jnp.square(jnp.maximum(_dot(h2, wup_ref[:, sl]), 0.0)).astype(BF16)
        f = f + _dot(hid, wdn_ref[sl, :])
    x = x + _rms(f, gpo_ref[...])

    ple = _dot(p_ref[...].astype(BF16), wple_ref[...]) * jax.nn.sigmoid(
        _dot(x.astype(BF16), wpg_ref[...]))
    o_ref[...] = x + _rms(ple, gpp_ref[...])


def _post_call(x, ya, yb, yc, gates, p, layer, consts):
    b, l, d = x.shape
    t = POST_TOKENS
    tok = lambda w: pl.BlockSpec((None, t, w), lambda bi, ti: (bi, ti, 0))
    p_spec = pl.BlockSpec((None, None, t, p.shape[3]), lambda bi, ti: (layer, bi, ti, 0))
    return pl.pallas_call(
        _post_kernel,
        grid=(b, l // t),
        in_specs=[tok(d), tok(ya.shape[2]), tok(yb.shape[2]), tok(yc.shape[2]),
                  tok(gates.shape[2]), p_spec] + [_const_spec(c.shape) for c in consts],
        out_specs=tok(d),
        out_shape=jax.ShapeDtypeStruct((b, l, d), F32),
        compiler_params=_params(2),
        name="merge_ffn_ple",
    )(x, ya, yb, yc, gates, p, *consts)


def _block_diag(w):
    nb, bi, bo = w.shape
    eye = jnp.eye(nb, dtype=w.dtype)
    return jnp.einsum("hij,hg->higj", w, eye).reshape(nb * bi, nb * bo)


def _pack_w_in(w_in):
    w_in = w_in.astype(BF16)
    d = w_in.shape[0]
    kiwi = w_in[:, SEG_KIWI:SEG_KIWI + IDX_DIM + IDX_HEADS]
    pad = jnp.zeros((d, LANES - IDX_DIM - IDX_HEADS), w_in.dtype)
    rest = w_in[:, SEG_KIWI + IDX_DIM + IDX_HEADS:]
    return jnp.concatenate([w_in[:, :SEG_KIWI], kiwi, pad, rest], axis=1)


def kernel(x, p, positions, g_pre_mix, w_in, conv_w, conv_b, w_rg_a, b_rg_a, w_rg_x, b_rg_x,
           lru_lambda, g_gmlp_v, w_spatial, b_spatial, w_branch, w_out, g_post_mix,
           g_pre_ffn, w_ffn_up, w_ffn_down, g_post_ffn, w_ple, w_ple_gate, g_post_ple):
    depth = w_in.shape[0]
    b, l, d = x.shape
    assert l % PROJ_TOKENS == 0 and l % DSA_KEYS == 0 and l % DSA_ROWS == 0
    assert l % POST_TOKENS == 0 and (b * l) % 1024 == 0 and DSA_KEYS >= INDEX_TOPK
    row = lambda a: a.reshape(1, -1)

    rope_c, rope_s = _rope_tables(positions)
    tril = jnp.tril(jnp.ones((CHUNK, CHUNK), dtype=bool))

    for i in range(depth):
        w_sp = jnp.where(tril[None], w_spatial[i], 0)
        w_sp = jnp.transpose(w_sp, (1, 0, 2)).reshape(CHUNK, GMLP_GROUPS * CHUNK).astype(BF16)
        b_sp = jnp.repeat(jnp.transpose(b_spatial[i]), GMLP_GROUP_DIM, axis=1)

        lru_consts = [conv_w[i], row(conv_b[i]),
                      _block_diag(w_rg_a[i]).astype(BF16), row(b_rg_a[i]),
                      _block_diag(w_rg_x[i]).astype(BF16), row(b_rg_x[i]), row(lru_lambda[i])]
        qt, k, vt, qit, ki, wit, yb, yc, gates = _proj_call(
            x, row(g_pre_mix[i]), _pack_w_in(w_in[i]), rope_c, rope_s,
            row(g_gmlp_v[i]), w_sp, b_sp, lru_consts)

        ya = _dsa_call(qit, wit, qt, ki, k, vt)

        consts = [w_branch[i].reshape(N_BRANCH * w_branch.shape[2], d).astype(BF16),
                  w_out[i].astype(BF16), row(g_post_mix[i]), row(g_pre_ffn[i]),
                  w_ffn_up[i].astype(BF16), w_ffn_down[i].astype(BF16), row(g_post_ffn[i]),
                  w_ple[i].astype(BF16), w_ple_gate[i].astype(BF16), row(g_post_ple[i])]
        x = _post_call(x, ya, yb, yc, gates, p, i, consts)
    return x
```

```python
import functools

import jax
import jax.numpy as jnp
import numpy as np
from jax import lax
from jax.experimental import pallas as pl
from jax.experimental.pallas import tpu as pltpu

F32 = jnp.float32
BF16 = jnp.bfloat16
I32 = jnp.int32
I16 = jnp.int16

N_HEADS = 8
HEAD_DIM = 64
ATTN_WIDTH = N_HEADS * HEAD_DIM
ROPE_DIM = HEAD_DIM // 4
ROPE_THETA = 500000.0
IDX_HEADS = 8
IDX_DIM = 64
INDEX_TOPK = 256
LRU_WIDTH = 512
LRU_BLOCKS = 8
CONV_WIDTH = 4
LRU_C = 8.0
GMLP_WIDTH = 512
GMLP_GROUPS = 8
GMLP_GROUP_DIM = GMLP_WIDTH // GMLP_GROUPS
CHUNK = 128
N_BRANCH = 3
EPS = 1e-6

LANES = 128
SUBLANES = 8
VMEM_LIMIT_BYTES = 56 * 1024 * 1024

INT_MIN = np.int32(-(2**31))
INT_MAX = np.int32(2**31 - 1)
MASK_BIAS = -1e30
Q_SCALE = float(np.log2(np.e)) * HEAD_DIM ** -0.5

PROJ_TOKENS = 256
DSA_ROWS = 256
DSA_KEYS = 512
DSA_SUB = 128
POST_TOKENS = 512
FFN_CHUNK = 1024

SEG_Q, SEG_K, SEG_V, SEG_QI = 0, 512, 1024, 1536
SEG_KIWI = 2048
SEG_XR, SEG_GR, SEG_ZU, SEG_ZV = 2176, 2688, 3200, 3712
SEG_GATE = 4224


def _rms(x, g):
    return x * lax.rsqrt(jnp.mean(x * x, axis=-1, keepdims=True) + EPS) * g


def _dot(a, b):
    return jnp.dot(a, b, preferred_element_type=F32)


def _params(n_grid):
    return pltpu.CompilerParams(
        dimension_semantics=("arbitrary",) * n_grid,
        vmem_limit_bytes=VMEM_LIMIT_BYTES)


def _const_spec(shape):
    zeros = (0,) * len(shape)
    return pl.BlockSpec(shape, lambda *_: zeros, pipeline_mode=pl.Buffered(1))


def _rope_table_kernel(pos_ref, invf_ref, c_ref, s_ref):
    ang = pos_ref[...].astype(F32) * invf_ref[...]
    lane = lax.broadcasted_iota(I32, ang.shape, 1) & (HEAD_DIM - 1)
    cos = jnp.cos(ang)
    sin = jnp.sin(ang)
    half = ROPE_DIM // 2
    c_ref[...] = jnp.where(lane < ROPE_DIM, cos, 1.0)
    s_ref[...] = jnp.where(lane < half, -sin, jnp.where(lane < ROPE_DIM, sin, 0.0))


def _rope_tables(positions):
    n = positions.size
    tile = 1024
    half = ROPE_DIM // 2
    inv_freq = ROPE_THETA ** (-jnp.arange(half, dtype=F32) * 2.0 / ROPE_DIM)
    inv_lane = jnp.tile(inv_freq, LANES // half).reshape(1, LANES)
    out = jax.ShapeDtypeStruct((n, LANES), F32)
    return pl.pallas_call(
        _rope_table_kernel,
        grid=(n // tile,),
        in_specs=[pl.BlockSpec((tile, 1), lambda i: (i, 0)),
                  pl.BlockSpec((1, LANES), lambda i: (0, 0))],
        out_specs=[pl.BlockSpec((tile, LANES), lambda i: (i, 0))] * 2,
        out_shape=[out, out],
        compiler_params=_params(1),
        name="rope_tables",
    )(positions.reshape(n, 1), inv_lane)


def _proj_kernel(x_ref, g_ref, w_ref, c_ref, s_ref, gv_ref, wsp_ref, bsp_ref,
                 cw_ref, cb_ref, wa_ref, ba_ref, wx_ref, bx_ref, lam_ref,
                 qt_ref, k_ref, vt_ref, qit_ref, ki_ref, wit_ref, yb_ref, yc_ref, gate_ref,
                 hist_ref, h_ref):
    t = x_ref.shape[0]
    h = _rms(x_ref[...], g_ref[...]).astype(BF16)
    cos = c_ref[...]
    sin = s_ref[...]
    lane = lax.broadcasted_iota(I32, (t, LANES), 1)
    first_half = (lane & (HEAD_DIM - 1)) < (ROPE_DIM // 2)

    def proj(start, width):
        return _dot(h, w_ref[:, start:start + width])

    def rope(p):
        cols = []
        for j in range(p.shape[1] // LANES):
            pj = p[:, j * LANES:(j + 1) * LANES]
            partner = jnp.where(first_half,
                                pltpu.roll(pj, LANES - ROPE_DIM // 2, 1),
                                pltpu.roll(pj, ROPE_DIM // 2, 1))
            cols.append(pj * cos + partner * sin)
        return cols[0] if len(cols) == 1 else jnp.concatenate(cols, axis=1)

    qt_ref[...] = (rope(proj(SEG_Q, ATTN_WIDTH)) * Q_SCALE).T.astype(BF16)
    k_ref[...] = rope(proj(SEG_K, ATTN_WIDTH)).astype(BF16)
    vt_ref[...] = proj(SEG_V, ATTN_WIDTH).T.astype(BF16)
    qit_ref[...] = rope(proj(SEG_QI, IDX_HEADS * IDX_DIM)).T.astype(BF16)

    kiwi = proj(SEG_KIWI, LANES)
    kiwi = jnp.where(lane < IDX_DIM, rope(kiwi), kiwi)
    ki_ref[...] = jnp.where(lane < IDX_DIM, kiwi, pltpu.roll(kiwi, IDX_DIM, 1)).astype(BF16)
    wit_ref[...] = kiwi.T[IDX_DIM:IDX_DIM + IDX_HEADS, :]

    _conv_rglru(proj(SEG_XR, LRU_WIDTH), proj(SEG_GR, LRU_WIDTH), cw_ref, cb_ref, wa_ref, ba_ref,
                wx_ref, bx_ref, lam_ref, yb_ref, hist_ref, h_ref)

    u = jax.nn.gelu(proj(SEG_ZU, GMLP_WIDTH))
    vn = _rms(jax.nn.gelu(proj(SEG_ZV, GMLP_WIDTH)), gv_ref[...]).astype(BF16)
    group = lax.broadcasted_iota(I32, (CHUNK, GMLP_WIDTH), 1) // GMLP_GROUP_DIM
    for c in range(t // CHUNK):
        vc = vn[c * CHUNK:(c + 1) * CHUNK, :]
        stack = jnp.concatenate(
            [jnp.where(group == g, vc, jnp.zeros_like(vc)) for g in range(GMLP_GROUPS)], axis=0)
        mixed = _dot(wsp_ref[...], stack) + bsp_ref[...]
        yc_ref[c * CHUNK:(c + 1) * CHUNK, :] = (u[c * CHUNK:(c + 1) * CHUNK, :] * mixed).astype(BF16)

    d = x_ref.shape[1]
    for n in range(N_BRANCH):
        gate_ref[:, n * d:(n + 1) * d] = jax.nn.sigmoid(proj(SEG_GATE + n * d, d)).astype(BF16)


def _proj_call(x, g_pre, w1, rope_c, rope_s, g_v, w_sp, b_sp, lru_consts):
    b, l, d = x.shape
    t = PROJ_TOKENS
    nt = l // t
    tok = lambda w: pl.BlockSpec((None, t, w), lambda bi, ti: (bi, ti, 0))
    tr = lambda w: pl.BlockSpec((None, w, t), lambda bi, ti: (bi, 0, ti))
    rope_spec = pl.BlockSpec((t, LANES), lambda bi, ti: (bi * nt + ti, 0))
    sds = lambda shape, dt: jax.ShapeDtypeStruct(shape, dt)
    return pl.pallas_call(
        _proj_kernel,
        grid=(b, nt),
        in_specs=[tok(d), _const_spec(g_pre.shape), _const_spec(w1.shape), rope_spec, rope_spec,
                  _const_spec(g_v.shape), _const_spec(w_sp.shape), _const_spec(b_sp.shape)]
                 + [_const_spec(c.shape) for c in lru_consts],
        out_specs=[tr(ATTN_WIDTH), tok(ATTN_WIDTH), tr(ATTN_WIDTH), tr(IDX_HEADS * IDX_DIM),
                   tok(LANES), tr(IDX_HEADS), tok(LRU_WIDTH), tok(GMLP_WIDTH),
                   tok(N_BRANCH * d)],
        out_shape=[sds((b, ATTN_WIDTH, l), BF16), sds((b, l, ATTN_WIDTH), BF16),
                   sds((b, ATTN_WIDTH, l), BF16), sds((b, IDX_HEADS * IDX_DIM, l), BF16),
                   sds((b, l, LANES), BF16), sds((b, IDX_HEADS, l), F32),
                   sds((b, l, LRU_WIDTH), BF16),
                   sds((b, l, GMLP_WIDTH), BF16), sds((b, l, N_BRANCH * d), BF16)],
        scratch_shapes=[pltpu.VMEM((SUBLANES, LRU_WIDTH), F32), pltpu.VMEM((SUBLANES, LRU_WIDTH), F32)],
        compiler_params=_params(2),
        name="in_proj",
    )(x, g_pre, w1, rope_c, rope_s, g_v, w_sp, b_sp, *lru_consts)


def _dsa_kernel(qit_ref, wit_ref, qt_ref, ki_ref, k_ref, vt_ref, o_ref,
                key_ref, hi_ref, lo_ref, s_ref, ot_ref, *, top_k):
    r = qt_ref.shape[1]
    l_keys = k_ref.shape[0]
    ck = DSA_KEYS
    sub = DSA_SUB
    nsub = ck // sub
    col0 = pl.program_id(1) * r
    nk = (col0 + r + ck - 1) // ck
    idx_scale = (IDX_DIM ** -0.5) * (IDX_HEADS ** -0.5)

    slab_row = lax.broadcasted_iota(I32, (LANES, r), 0)

    def head_weights(ref, h):
        slab = ref[(h // 2) * LANES:(h // 2 + 1) * LANES, :]
        keep = (slab_row < HEAD_DIM) if h % 2 == 0 else (slab_row >= HEAD_DIM)
        return jnp.where(keep, slab, jnp.zeros_like(slab))

    def chunk_loop(body, init):
        def wrapped(kc, carry):
            return body(pl.multiple_of(kc * ck, ck), carry)
        return lax.fori_loop(0, nk, wrapped, init)

    wit = wit_ref[...] * idx_scale
    w_idx = [head_weights(qit_ref, h) for h in range(IDX_HEADS)]
    qpos = col0 + lax.broadcasted_iota(I32, (sub, r), 1)
    krow = lax.broadcasted_iota(I32, (sub, r), 0)

    def score_chunk(k0, carry):
        for sb in range(nsub):
            k1 = k0 + sb * sub
            ki = ki_ref[pl.ds(k1, sub), :]
            acc = jnp.zeros((sub, r), F32)
            for h in range(IDX_HEADS):
                acc = acc + wit[h:h + 1, :] * jnp.maximum(_dot(ki, w_idx[h]), 0.0)
            bits = pltpu.bitcast(acc, I32)
            key = bits ^ ((bits >> 31) & INT_MAX)
            key = jnp.where(k1 + krow <= qpos, key, INT_MIN)
            key_ref[pl.ds(k1, sub), :] = key
            hi_ref[pl.ds(k1, sub), :] = (key >> 16).astype(I16)
            lo_ref[pl.ds(k1, sub), :] = ((key & 0xFFFF) - 32768).astype(I16)
        return carry

    chunk_loop(score_chunk, 0)

    n_acc = 4

    def count(ref, rows, pred):
        dt = ref.dtype
        one, zero = jnp.ones((), dt), jnp.zeros((), dt)

        def body(k0, accs):
            accs = list(accs)
            chunk = ref[pl.ds(k0, ck), :]
            for j in range(ck // rows):
                blk = chunk[j * rows:(j + 1) * rows, :]
                accs[j % n_acc] = accs[j % n_acc] + jnp.where(pred(blk, k0 + j * rows), one, zero)
            return tuple(accs)

        accs = chunk_loop(body, tuple(jnp.zeros((rows, r), dt) for _ in range(n_acc)))
        tot = accs[0].astype(I32)
        for a in accs[1:]:
            tot = tot + a.astype(I32)
        return jnp.sum(tot, axis=0, keepdims=True)

    def bisect16(ref, need):
        rows = 2 * SUBLANES

        def step(b, lo):
            cand = lo + lax.shift_left(jnp.int32(1), 15 - b)
            cand16 = jnp.broadcast_to(cand.astype(I16), (rows, r))
            n_ge = count(ref, rows, lambda blk, _: blk >= cand16)
            return jnp.where(n_ge >= need, cand, lo)

        return lax.fori_loop(0, 16, step, jnp.full((1, r), -32768, I32))

    thr_hi = bisect16(hi_ref, top_k)
    thr_hi16 = jnp.broadcast_to(thr_hi.astype(I16), (2 * SUBLANES, r))
    n_hi_gt = count(hi_ref, 2 * SUBLANES, lambda blk, _: blk > thr_hi16)

    def mask_low(k0, carry):
        sl = pl.ds(k0, ck)
        match = hi_ref[sl, :] == jnp.broadcast_to(thr_hi.astype(I16), (ck, r))
        lo_ref[sl, :] = jnp.where(match, lo_ref[sl, :], jnp.int16(-32768))
        return carry

    chunk_loop(mask_low, 0)
    thr_lo = bisect16(lo_ref, top_k - n_hi_gt)
    thr = thr_hi * 65536 + (thr_lo + 32768)

    thr8 = jnp.broadcast_to(thr, (SUBLANES, r))
    n_gt = count(key_ref, SUBLANES, lambda blk, _: blk > thr8)
    n_eq = count(key_ref, SUBLANES, lambda blk, _: blk == thr8)
    need = top_k - n_gt
    real = thr > INT_MIN
    has_tie = jnp.logical_and(real, n_eq > need)
    row8 = lax.broadcasted_iota(I32, (SUBLANES, r), 0)

    def tie_search():
        n_bits = max(1, int(l_keys - 1).bit_length())

        def pos_step(b, last):
            cand = last + lax.shift_left(jnp.int32(1), n_bits - 1 - b)
            n_before = count(key_ref, SUBLANES,
                             lambda blk, p0: jnp.logical_and(blk == thr8, p0 + row8 < cand))
            return jnp.where(n_before < need, cand, last)

        return lax.fori_loop(0, n_bits, pos_step, jnp.zeros((1, r), I32))

    any_tie = jnp.max(jnp.where(has_tie, 1.0, 0.0)) > 0.0
    key_row = lax.broadcasted_iota(I32, (ck, r), 0)

    def write_bias(bias_of):
        def bias_chunk(k0, carry):
            sl = pl.ds(k0, ck)
            key_ref[sl, :] = pltpu.bitcast(bias_of(key_ref[sl, :], k0), I32)
            return carry
        chunk_loop(bias_chunk, 0)

    def keep(selected):
        return jnp.where(selected, 0.0, MASK_BIAS).astype(F32)

    @pl.when(jnp.logical_not(any_tie))
    def _():
        thr_b = jnp.broadcast_to(jnp.where(real, thr, INT_MIN + 1), (ck, r))
        write_bias(lambda key, k0: keep(key >= thr_b))

    @pl.when(any_tie)
    def _():
        last = jnp.where(has_tie, tie_search(), jnp.where(real, INT_MAX, -1))
        thr_b = jnp.broadcast_to(thr, (ck, r))
        last_b = jnp.broadcast_to(last, (ck, r))
        write_bias(lambda key, k0: jnp.where(key == thr_b, keep(k0 + key_row <= last_b),
                                             keep(key > thr_b)))

    w_q = [head_weights(qt_ref, h) for h in range(N_HEADS)]
    pack = 2 * SUBLANES

    def fold8(a):
        out = a[0:SUBLANES, :]
        for j in range(1, a.shape[0] // SUBLANES):
            out = jnp.maximum(out, a[j * SUBLANES:(j + 1) * SUBLANES, :])
        return out

    def logits_chunk(k0, ms):
        ms = list(ms)
        for sb in range(nsub):
            k1 = k0 + sb * sub
            bias = pltpu.bitcast(key_ref[pl.ds(k1, sub), :], F32)
            for h in range(N_HEADS):
                pair = (h // 2) * LANES
                s = _dot(k_ref[pl.ds(k1, sub), pair:pair + LANES], w_q[h]) + bias
                s_ref[h, pl.ds(k1, sub), :] = s.astype(BF16)
                ms[h] = jnp.maximum(ms[h], fold8(s))
        return tuple(ms)

    ms = chunk_loop(logits_chunk,
                    tuple(jnp.full((SUBLANES, r), -jnp.inf, F32) for _ in range(N_HEADS)))
    ms = [jnp.broadcast_to(jnp.max(m, axis=0, keepdims=True).astype(BF16), (ck, r)) for m in ms]

    ot_ref[...] = jnp.zeros_like(ot_ref)
    ones_rows = jnp.ones((pack, ck), BF16)

    def pv_chunk(k0, carry):
        sl = pl.ds(k0, ck)
        for h in range(N_HEADS):
            p = jnp.exp2(s_ref[h, sl, :] - ms[h])
            lhs = jnp.concatenate([vt_ref[h * HEAD_DIM:(h + 1) * HEAD_DIM, sl], ones_rows], axis=0)
            ot_ref[h] += _dot(lhs, p)
        return carry

    chunk_loop(pv_chunk, 0)
    outs = []
    for h in range(N_HEADS):
        acc = ot_ref[h]
        outs.append(acc[0:HEAD_DIM, :] * (1.0 / acc[HEAD_DIM:HEAD_DIM + 1, :]))
    o_ref[...] = jnp.concatenate(outs, axis=0).T.astype(BF16)


def _dsa_call(qit, wit, qt, ki, k, vt):
    b, l, _ = k.shape
    r = DSA_ROWS
    top_k = min(INDEX_TOPK, l // 4)
    col = lambda h: pl.BlockSpec((None, h, r), lambda bi, ti: (bi, 0, ti))
    full = lambda h, w: pl.BlockSpec((None, h, w), lambda bi, ti: (bi, 0, 0),
                                     pipeline_mode=pl.Buffered(1))
    return pl.pallas_call(
        functools.partial(_dsa_kernel, top_k=top_k),
        grid=(b, l // r),
        in_specs=[col(IDX_HEADS * IDX_DIM), col(IDX_HEADS), col(ATTN_WIDTH),
                  full(l, LANES), full(l, ATTN_WIDTH), full(ATTN_WIDTH, l)],
        out_specs=pl.BlockSpec((None, r, ATTN_WIDTH), lambda bi, ti: (bi, ti, 0)),
        out_shape=jax.ShapeDtypeStruct((b, l, ATTN_WIDTH), BF16),
        scratch_shapes=[pltpu.VMEM((l, r), I32), pltpu.VMEM((l, r), I16), pltpu.VMEM((l, r), I16),
                        pltpu.VMEM((N_HEADS, l, r), BF16),
                        pltpu.VMEM((N_HEADS, HEAD_DIM + 2 * SUBLANES, r), F32)],
        compiler_params=_params(2),
        name="dsa_attention",
    )(qit, wit, qt, ki, k, vt)


def _conv_rglru(x, gr, cw_ref, cb_ref, wa_ref, ba_ref, wx_ref, bx_ref, lam_ref,
                y_ref, hist_ref, h_ref):
    t = x.shape[0]

    @pl.when(pl.program_id(1) == 0)
    def _():
        hist_ref[...] = jnp.zeros_like(hist_ref)
        h_ref[...] = jnp.zeros_like(h_ref)

    hist = hist_ref[...]
    row = lax.broadcasted_iota(I32, x.shape, 0)
    row8 = lax.broadcasted_iota(I32, hist.shape, 0)
    cw = cw_ref[...]
    conv = x * cw[CONV_WIDTH - 1:CONV_WIDTH, :] + cb_ref[...]
    for d in range(1, CONV_WIDTH):
        shifted = pltpu.roll(x, d, 0)
        head = jnp.where(row8 < d, pltpu.roll(hist, d, 0), shifted[0:SUBLANES, :])
        shifted = jnp.concatenate([head, shifted[SUBLANES:, :]], axis=0)
        conv = conv + shifted * cw[CONV_WIDTH - 1 - d:CONV_WIDTH - d, :]
    hist_ref[...] = x[t - SUBLANES:, :]

    cb16 = conv.astype(BF16)
    rg = jax.nn.sigmoid(_dot(cb16, wa_ref[...]) + ba_ref[...])
    ig = jax.nn.sigmoid(_dot(cb16, wx_ref[...]) + bx_ref[...])
    log_a = -LRU_C * rg * jax.nn.softplus(-lam_ref[...])
    a = jnp.exp(log_a)
    inp = jnp.sqrt(-jnp.tanh(log_a) * (a * a + 1.0)) * (ig * conv)

    in_group = row & (SUBLANES - 1)
    d = 1
    while d < SUBLANES:
        ok = in_group >= d
        inp = jnp.where(ok, a * pltpu.roll(inp, d, 0) + inp, inp)
        a = jnp.where(ok, a * pltpu.roll(a, d, 0), a)
        d *= 2
    carry = h_ref[0:1, :]
    gelu_g = jax.nn.gelu(gr)
    pack = 2 * SUBLANES
    for g in range(t // pack):
        halves = []
        for half in range(2):
            rows = slice(g * pack + half * SUBLANES, g * pack + (half + 1) * SUBLANES)
            hg = inp[rows, :] + a[rows, :] * carry
            carry = hg[SUBLANES - 1:SUBLANES, :]
            halves.append(hg * gelu_g[rows, :])
        y_ref[g * pack:(g + 1) * pack, :] = jnp.concatenate(halves, axis=0).astype(BF16)
    h_ref[...] = jnp.broadcast_to(carry, h_ref.shape)


def _post_kernel(x_ref, ya_ref, yb_ref, yc_ref, gate_ref, p_ref,
                 wb_ref, wo_ref, gpm_ref, gpf_ref, wup_ref, wdn_ref, gpo_ref,
                 wple_ref, wpg_ref, gpp_ref, o_ref):
    d = x_ref.shape[1]
    w = ya_ref.shape[1]
    x = x_ref[...]
    merged = jnp.zeros(x.shape, F32)
    for n, y_ref in enumerate((ya_ref, yb_ref, yc_ref)):
        branch = _dot(y_ref[...], wb_ref[n * w:(n + 1) * w, :])
        merged = merged + gate_ref[:, n * d:(n + 1) * d].astype(F32) * branch
    x = x + _rms(_dot(merged.astype(BF16), wo_ref[...]), gpm_ref[...])

    h2 = _rms(x, gpf_ref[...]).astype(BF16)
    f = jnp.zeros(x.shape, F32)
    for c in range(wup_ref.shape[1] // FFN_CHUNK):
        sl = slice(c * FFN_CHUNK, (c + 1) * FFN_CHUNK)
        hid = jnp.square(jnp.maximum(_dot(h2, wup_ref[:, sl]), 0.0)).astype(BF16)
        f = f + _dot(hid, wdn_ref[sl, :])
    x = x + _rms(f, gpo_ref[...])

    ple = _dot(p_ref[...].astype(BF16), wple_ref[...]) * jax.nn.sigmoid(
        _dot(x.astype(BF16), wpg_ref[...]))
    o_ref[...] = x + _rms(ple, gpp_ref[...])


def _post_call(x, ya, yb, yc, gates, p, layer, consts):
    b, l, d = x.shape
    t = POST_TOKENS
    tok = lambda w: pl.BlockSpec((None, t, w), lambda bi, ti: (bi, ti, 0))
    p_spec = pl.BlockSpec((None, None, t, p.shape[3]), lambda bi, ti: (layer, bi, ti, 0))
    return pl.pallas_call(
        _post_kernel,
        grid=(b, l // t),
        in_specs=[tok(d), tok(ya.shape[2]), tok(yb.shape[2]), tok(yc.shape[2]),
                  tok(gates.shape[2]), p_spec] + [_const_spec(c.shape) for c in consts],
        out_specs=tok(d),
        out_shape=jax.ShapeDtypeStruct((b, l, d), F32),
        compiler_params=_params(2),
        name="merge_ffn_ple",
    )(x, ya, yb, yc, gates, p, *consts)


def _block_diag(w):
    nb, bi, bo = w.shape
    eye = jnp.eye(nb, dtype=w.dtype)
    return jnp.einsum("hij,hg->higj", w, eye).reshape(nb * bi, nb * bo)


def _pack_w_in(w_in):
    w_in = w_in.astype(BF16)
    d = w_in.shape[0]
    kiwi = w_in[:, SEG_KIWI:SEG_KIWI + IDX_DIM + IDX_HEADS]
    pad = jnp.zeros((d, LANES - IDX_DIM - IDX_HEADS), w_in.dtype)
    rest = w_in[:, SEG_KIWI + IDX_DIM + IDX_HEADS:]
    return jnp.concatenate([w_in[:, :SEG_KIWI], kiwi, pad, rest], axis=1)


def kernel(x, p, positions, g_pre_mix, w_in, conv_w, conv_b, w_rg_a, b_rg_a, w_rg_x, b_rg_x,
           lru_lambda, g_gmlp_v, w_spatial, b_spatial, w_branch, w_out, g_post_mix,
           g_pre_ffn, w_ffn_up, w_ffn_down, g_post_ffn, w_ple, w_ple_gate, g_post_ple):
    depth = w_in.shape[0]
    b, l, d = x.shape
    assert l % PROJ_TOKENS == 0 and l % DSA_KEYS == 0 and l % DSA_ROWS == 0
    assert l % POST_TOKENS == 0 and (b * l) % 1024 == 0 and DSA_KEYS >= INDEX_TOPK
    row = lambda a: a.reshape(1, -1)

    rope_c, rope_s = _rope_tables(positions)
    tril = jnp.tril(jnp.ones((CHUNK, CHUNK), dtype=bool))

    for i in range(depth):
        w_sp = jnp.where(tril[None], w_spatial[i], 0)
        w_sp = jnp.transpose(w_sp, (1, 0, 2)).reshape(CHUNK, GMLP_GROUPS * CHUNK).astype(BF16)
        b_sp = jnp.repeat(jnp.transpose(b_spatial[i]), GMLP_GROUP_DIM, axis=1)

        lru_consts = [conv_w[i], row(conv_b[i]),
                      _block_diag(w_rg_a[i]).astype(BF16), row(b_rg_a[i]),
                      _block_diag(w_rg_x[i]).astype(BF16), row(b_rg_x[i]), row(lru_lambda[i])]
        qt, k, vt, qit, ki, wit, yb, yc, gates = _proj_call(
            x, row(g_pre_mix[i]), _pack_w_in(w_in[i]), rope_c, rope_s,
            row(g_gmlp_v[i]), w_sp, b_sp, lru_consts)

        ya = _dsa_call(qit, wit, qt, ki, k, vt)

        consts = [w_branch[i].reshape(N_BRANCH * w_branch.shape[2], d).astype(BF16),
                  w_out[i].astype(BF16), row(g_post_mix[i]), row(g_pre_ffn[i]),
                  w_ffn_up[i].astype(BF16), w_ffn_down[i].astype(BF16), row(g_post_ffn[i]),
                  w_ple[i].astype(BF16), w_ple_gate[i].astype(BF16), row(g_post_ple[i])]
        x = _post_call(x, ya, yb, yc, gates, p, i, consts)
    return x
```
